```python
import math
import jax, jax.numpy as jnp
from jax import lax
import numpy as np

D_MODEL = 1024
BATCH = 2
SEQ = 8192
DEPTH = 2
DEC_BATCH = 128
DEC_SEQ = 1
PAST_LEN = 16384
PAGE_SIZE = 128

WINDOW = 128
A_HEADS = 8
A_KV_HEADS = 2
A_HEAD_DIM = 64
A_GROUP = A_HEADS // A_KV_HEADS
A_WIDTH = A_HEADS * A_HEAD_DIM
B_HEADS = 4
B_HEAD_DIM = 128
B_WIDTH = B_HEADS * B_HEAD_DIM
CONV_W = 4
GDN_CHUNK = 64
POOL_WINDOWS = (2, 4, 8, 16)
POOL_GROUPS = 4
C_WIDTH = 512
POOL_GROUP_DIM = C_WIDTH // POOL_GROUPS
POOL_HIST = max(POOL_WINDOWS) - 1
N_BRANCHES = 3
IN_SPLIT_SIZES = (A_WIDTH, A_KV_HEADS * A_HEAD_DIM, A_KV_HEADS * A_HEAD_DIM, 3 * B_WIDTH,
                  B_WIDTH, B_HEADS, B_HEADS, C_WIDTH, N_BRANCHES * D_MODEL)
IN_COLS = sum(IN_SPLIT_SIZES)
N_GROUPS = 4
EXPERTS_PER_GROUP = 8
N_EXPERTS = N_GROUPS * EXPERTS_PER_GROUP
TOP_K = 2
EXPERT_FF = 512
MOE_BLOCK = 128
DN_ALPHA = (2.0 * DEPTH) ** 0.25
DN_BETA = (8.0 * DEPTH) ** -0.25
LN_EPS = 1e-5
RMS_EPS = 1e-6
NEG_INF = -1e30

kernel_name = "hybrid_swa_gdn_pool_hmoe_step"

F32 = jnp.float32


def split_points():
    return [int(v) for v in np.cumsum(IN_SPLIT_SIZES)[:-1]]


def layer_norm(x, g, b):
    xf = x.astype(F32)
    mu = xf.mean(-1, keepdims=True)
    var = jnp.square(xf - mu).mean(-1, keepdims=True)
    return ((xf - mu) * lax.rsqrt(var + LN_EPS) * g.astype(F32) + b.astype(F32)).astype(x.dtype)


def rms_norm(x, g):
    xf = x.astype(F32)
    return xf * lax.rsqrt(jnp.mean(xf * xf, -1, keepdims=True) + RMS_EPS) * g.astype(F32)


def l2norm(x):
    xf = x.astype(F32)
    return xf * lax.rsqrt(jnp.sum(xf * xf, -1, keepdims=True) + RMS_EPS)


def alibi_slopes():
    return jnp.asarray([2.0 ** (-8.0 * (h + 1) / A_HEADS) for h in range(A_HEADS)], F32)


def window_attention(q, k, v, k_prev, v_prev, prev_valid, sinks):
    bsz, L = q.shape[:2]
    qb = WINDOW if L % WINDOW == 0 else L
    nb = L // qb
    kf = jnp.concatenate([k_prev, k], axis=1)
    vf = jnp.concatenate([v_prev, v], axis=1)
    valid = jnp.concatenate([jnp.full((WINDOW,), prev_valid), jnp.ones((L,), bool)])

    def band(t):
        if nb == 1:
            return t[:, None]
        tr = t.reshape((t.shape[0], nb + 1, WINDOW) + t.shape[2:])
        return jnp.concatenate([tr[:, :-1], tr[:, 1:]], axis=2)

    kb, vb = band(kf), band(vf)
    vmask = band(valid[None])[0]
    qg = q.reshape(bsz, nb, qb, A_KV_HEADS, A_GROUP, A_HEAD_DIM)
    s = jnp.einsum('bnqkgd,bnskd->bnkgqs', qg, kb, preferred_element_type=F32) * (A_HEAD_DIM ** -0.5)
    dist = WINDOW + jnp.arange(qb)[:, None] - jnp.arange(WINDOW + qb)[None, :]
    mask = (dist >= 0) & (dist <= WINDOW) & vmask[:, None, :]
    bias = -alibi_slopes().reshape(A_KV_HEADS, A_GROUP, 1, 1) * dist.astype(F32)
    s = jnp.where(mask[None, :, None, None], s + bias, NEG_INF)
    sink = jnp.broadcast_to(sinks.astype(F32).reshape(A_KV_HEADS, A_GROUP, 1, 1), s.shape[:-1] + (1,))
    p = jax.nn.softmax(jnp.concatenate([s, sink], axis=-1), axis=-1)[..., :-1]
    o = jnp.einsum('bnkgqs,bnskd->bnqkgd', p.astype(v.dtype), vb)
    return o.reshape(bsz, L, A_WIDTH), kf[:, -WINDOW:], vf[:, -WINDOW:]


def gated_delta_rule(q, k, v, g, beta, s0):
    bsz, L, H, _ = q.shape
    dv = v.shape[-1]
    c = min(GDN_CHUNK, L)
    n = -(-L // c)
    pad = n * c - L

    def prep(t):
        t = t.astype(F32)
        t = jnp.pad(t, [(0, 0), (0, pad)] + [(0, 0)] * (t.ndim - 2))
        t = t.reshape((bsz, n, c) + t.shape[2:])
        return jnp.moveaxis(t, 3, 1)

    qc, kc, vc, gc, bc = prep(q), prep(k), prep(v), prep(g), prep(beta)
    G = jnp.cumsum(gc, axis=-1)
    idx = jnp.arange(c)
    incl = idx[:, None] >= idx[None, :]
    strict = idx[:, None] > idx[None, :]
    diff = G[..., :, None] - G[..., None, :]
    decay = jnp.where(incl, jnp.exp(jnp.where(incl, diff, 0.0)), 0.0)
    kb = kc * bc[..., None]
    vb = vc * bc[..., None]
    m = jnp.where(strict, jnp.einsum('bhnid,bhnjd->bhnij', kb, kc) * decay, 0.0)
    a_mat = m + jnp.eye(c, dtype=F32)
    u = lax.linalg.triangular_solve(a_mat, vb, left_side=True, lower=True, unit_diagonal=True)
    w = lax.linalg.triangular_solve(a_mat, kb * jnp.exp(G)[..., None], left_side=True, lower=True,
                                    unit_diagonal=True)
    aqk = jnp.einsum('bhnid,bhnjd->bhnij', qc, kc) * decay
    qg = qc * jnp.exp(G)[..., None]
    kd = kc * jnp.exp(G[..., -1:] - G)[..., None]
    glast = jnp.exp(G[..., -1])

    def step(s, xs):
        qg_i, kd_i, u_i, w_i, aqk_i, gl_i = xs
        v_new = u_i - jnp.einsum('bhcd,bhde->bhce', w_i, s)
        o = jnp.einsum('bhcd,bhde->bhce', qg_i, s) + jnp.einsum('bhij,bhje->bhie', aqk_i, v_new)
        s = s * gl_i[..., None, None] + jnp.einsum('bhcd,bhce->bhde', kd_i, v_new)
        return s, o

    xs = tuple(jnp.moveaxis(t, 2, 0) for t in (qg, kd, u, w, aqk, glast))
    s_fin, o = lax.scan(step, s0.astype(F32), xs)
    o = jnp.moveaxis(o, 0, 2).reshape(bsz, H, n * c, dv)[:, :, :L]
    return jnp.moveaxis(o, 1, 2), s_fin


def gdn_branch(qkv, z, a, b, conv_buf, s0, conv_w, a_log, dt_bias, norm_g):
    bsz, L, _ = qkv.shape
    xp = jnp.concatenate([conv_buf, qkv], axis=1)
    conv = sum(xp[:, j:j + L] * conv_w[j] for j in range(CONV_W))
    conv = jax.nn.silu(conv)
    q, k, v = jnp.split(conv, 3, axis=-1)
    shp = (bsz, L, B_HEADS, B_HEAD_DIM)
    q = l2norm(q.reshape(shp)) * (B_HEAD_DIM ** -0.5)
    k = l2norm(k.reshape(shp))
    v = v.reshape(shp)
    beta = jax.nn.sigmoid(b.astype(F32))
    g = -jnp.exp(a_log.astype(F32)) * jax.nn.softplus(a.astype(F32) + dt_bias.astype(F32))
    o, s_new = gated_delta_rule(q, k, v, g, beta, s0)
    o = rms_norm(o, norm_g) * jax.nn.silu(z.reshape(shp).astype(F32))
    return o.reshape(bsz, L, B_WIDTH).astype(qkv.dtype), xp[:, -(CONV_W - 1):], s_new


def pool_branch(p, prev, start_pos, pool_w, pool_scale):
    bsz, L, _ = p.shape
    xp = jnp.concatenate([prev, p], axis=1).astype(F32)
    cs = jnp.concatenate([jnp.zeros((bsz, 1, C_WIDTH), F32), jnp.cumsum(xp, axis=1)], axis=1)
    cnt_pos = start_pos + jnp.arange(L) + 1
    means = []
    for gi, w in enumerate(POOL_WINDOWS):
        lo, hi = gi * POOL_GROUP_DIM, (gi + 1) * POOL_GROUP_DIM
        wsum = (cs[:, POOL_HIST + 1:POOL_HIST + 1 + L, lo:hi]
                - cs[:, POOL_HIST + 1 - w:POOL_HIST + 1 - w + L, lo:hi])
        cnt = jnp.minimum(cnt_pos, w).astype(F32)
        means.append(wsum / cnt[None, :, None])
    mean = jnp.concatenate(means, axis=-1)
    d = (mean - xp[:, POOL_HIST:]).reshape(bsz, L, POOL_GROUPS, POOL_GROUP_DIM)
    out = jnp.einsum('blgc,gce->blge', d, pool_w.astype(F32)).reshape(bsz, L, C_WIDTH)
    out = out * pool_scale.astype(F32)
    return out.astype(p.dtype), xp[:, -POOL_HIST:].astype(p.dtype)


def hier_moe(h, rg_w, rg_b, re_w, re_b, w_gate, w_up, w_down):
    shp = h.shape
    xt = h.reshape(-1, D_MODEL)
    n = xt.shape[0]
    glog = (xt @ rg_w).astype(F32) + rg_b.astype(F32)
    gsel = jnp.argmax(glog, axis=-1)
    gw = jnp.take_along_axis(jax.nn.softmax(glog, axis=-1), gsel[:, None], 1)[:, 0]
    elog_all = jnp.einsum('nd,gde->nge', xt, re_w).astype(F32) + re_b.astype(F32)
    elog = jnp.take_along_axis(elog_all, gsel[:, None, None], 1)[:, 0]
    topv, topi = lax.top_k(elog, TOP_K)
    wts = gw[:, None] * jax.nn.softmax(topv, axis=-1)
    eid = (gsel[:, None] * EXPERTS_PER_GROUP + topi).reshape(-1).astype(jnp.int32)
    tok = jnp.repeat(jnp.arange(n, dtype=jnp.int32), TOP_K)
    m = n * TOP_K
    order = jnp.argsort(eid)
    eid_s, tok_s, w_s = eid[order], tok[order], wts.reshape(-1)[order]
    sizes = jnp.bincount(eid, length=N_EXPERTS).astype(jnp.int32)
    padded = (sizes + MOE_BLOCK - 1) // MOE_BLOCK * MOE_BLOCK
    start_sorted = jnp.cumsum(sizes) - sizes
    end_padded = jnp.cumsum(padded)
    start_padded = end_padded - padded
    dest = start_padded[eid_s] + jnp.arange(m, dtype=jnp.int32) - start_sorted[eid_s]
    cap = -(-m // MOE_BLOCK) * MOE_BLOCK + N_EXPERTS * MOE_BLOCK
    n_blocks = cap // MOE_BLOCK
    buf_tok = jnp.zeros((cap,), jnp.int32).at[dest].set(tok_s)
    buf_w = jnp.zeros((cap,), F32).at[dest].set(w_s)
    blk_exp = jnp.minimum(jnp.searchsorted(end_padded, jnp.arange(n_blocks, dtype=jnp.int32) * MOE_BLOCK,
                                           side='right'), N_EXPERTS - 1)
    xb = xt[buf_tok].reshape(n_blocks, MOE_BLOCK, D_MODEL)

    def expert_block(args):
        xblk, e = args
        hid = jax.nn.silu(xblk @ w_gate[e]) * (xblk @ w_up[e])
        return hid @ w_down[e]

    yb = lax.map(expert_block, (xb, blk_exp))
    y = jnp.zeros((n, D_MODEL), F32).at[buf_tok].add(yb.reshape(cap, D_MODEL).astype(F32) * buf_w[:, None])
    return y.reshape(shp).astype(h.dtype)


def trunk_layer(h, k_prev, v_prev, prev_valid, conv_buf, s0, pool_prev, start_pos, lw):
    bsz, L, _ = h.shape
    proj = jnp.einsum('bld,de->ble', h, lw['w_in'])
    aq, ak, av, bqkv, bz, ba, bb, cp, gate_raw = jnp.split(proj, split_points(), axis=-1)
    oa, new_k, new_v = window_attention(
        aq.reshape(bsz, L, A_HEADS, A_HEAD_DIM), ak.reshape(bsz, L, A_KV_HEADS, A_HEAD_DIM),
        av.reshape(bsz, L, A_KV_HEADS, A_HEAD_DIM), k_prev, v_prev, prev_valid, lw['attn_sinks'])
    ob, new_conv, new_s = gdn_branch(bqkv, bz, ba, bb, conv_buf, s0, lw['gdn_conv_w'], lw['gdn_a_log'],
                                     lw['gdn_dt_bias'], lw['gdn_norm_g'])
    oc, new_pool = pool_branch(cp, pool_prev, start_pos, lw['pool_w'], lw['pool_scale'])
    gates = jax.nn.sigmoid(gate_raw.astype(F32)).reshape(bsz, L, N_BRANCHES, D_MODEL)
    merged = (gates[:, :, 0] * (oa @ lw['w_branch_a']).astype(F32)
              + gates[:, :, 1] * (ob @ lw['w_branch_b']).astype(F32)
              + gates[:, :, 2] * (oc @ lw['w_branch_c']).astype(F32))
    mix = merged.astype(h.dtype) @ lw['w_o']
    h1 = layer_norm(DN_ALPHA * h + mix, lw['ln1_g'], lw['ln1_b'])
    ffn = hier_moe(h1, lw['router_group_w'], lw['router_group_b'], lw['router_expert_w'],
                   lw['router_expert_b'], lw['w_gate'], lw['w_up'], lw['w_down'])
    h2 = layer_norm(DN_ALPHA * h1 + ffn, lw['ln2_g'], lw['ln2_b'])
    return h2, new_k, new_v, new_conv, new_s, new_pool


def setup_inputs(seed: int = 0) -> dict:
    key = jax.random.key(seed)
    ks = list(jax.random.split(key, 32))

    def nrm(shape, scale):
        return jax.random.normal(ks.pop(), shape, F32) * scale

    w_cache = min(WINDOW, PAST_LEN)
    x_prompt = nrm((BATCH, SEQ, D_MODEL), 1.0)
    x_sample = nrm((DEC_BATCH, DEC_SEQ, D_MODEL), 1.0)
    cache_attn_k = nrm((DEPTH, DEC_BATCH, w_cache, A_KV_HEADS, A_HEAD_DIM), 1.0)
    cache_attn_v = nrm((DEPTH, DEC_BATCH, w_cache, A_KV_HEADS, A_HEAD_DIM), 1.0)
    state_gdn_conv = nrm((DEPTH, DEC_BATCH, CONV_W - 1, 3 * B_WIDTH), 1.0)
    state_gdn_rec = nrm((DEPTH, DEC_BATCH, B_HEADS, B_HEAD_DIM, B_HEAD_DIM), 0.1)
    state_pool = nrm((DEPTH, DEC_BATCH, POOL_HIST, C_WIDTH), 1.0)
    w_in = nrm((DEPTH, D_MODEL, IN_COLS), D_MODEL ** -0.5)
    attn_sinks = nrm((DEPTH, A_HEADS), 0.5)
    gdn_conv_w = nrm((DEPTH, CONV_W, 3 * B_WIDTH), 0.5)
    gdn_a_log = jnp.log(jax.random.uniform(ks.pop(), (DEPTH, B_HEADS), F32, 1.0, 16.0))
    dt = jnp.exp(jax.random.uniform(ks.pop(), (DEPTH, B_HEADS), F32, math.log(1e-3), math.log(1e-1)))
    gdn_dt_bias = dt + jnp.log(-jnp.expm1(-dt))
    gdn_norm_g = 1.0 + nrm((DEPTH, B_HEAD_DIM), 0.02)
    pool_w = nrm((DEPTH, POOL_GROUPS, POOL_GROUP_DIM, POOL_GROUP_DIM), POOL_GROUP_DIM ** -0.5)
    pool_scale = 1.0 + nrm((DEPTH, C_WIDTH), 0.02)
    w_branch_a = nrm((DEPTH, A_WIDTH, D_MODEL), A_WIDTH ** -0.5)
    w_branch_b = nrm((DEPTH, B_WIDTH, D_MODEL), B_WIDTH ** -0.5)
    w_branch_c = nrm((DEPTH, C_WIDTH, D_MODEL), C_WIDTH ** -0.5)
    w_o = nrm((DEPTH, D_MODEL, D_MODEL), D_MODEL ** -0.5 * DN_BETA)
    ln1_g = 1.0 + nrm((DEPTH, D_MODEL), 0.02)
    ln1_b = nrm((DEPTH, D_MODEL), 0.02)
    router_group_w = nrm((DEPTH, D_MODEL, N_GROUPS), D_MODEL ** -0.5)
    router_group_b = nrm((DEPTH, N_GROUPS), 0.01)
    router_expert_w = nrm((DEPTH, N_GROUPS, D_MODEL, EXPERTS_PER_GROUP), D_MODEL ** -0.5)
    router_expert_b = nrm((DEPTH, N_GROUPS, EXPERTS_PER_GROUP), 0.01)
    w_gate = nrm((DEPTH, N_EXPERTS, D_MODEL, EXPERT_FF), D_MODEL ** -0.5)
    w_up = nrm((DEPTH, N_EXPERTS, D_MODEL, EXPERT_FF), D_MODEL ** -0.5)
    w_down = nrm((DEPTH, N_EXPERTS, EXPERT_FF, D_MODEL), EXPERT_FF ** -0.5 * DN_BETA)
    ln2_g = 1.0 + nrm((DEPTH, D_MODEL), 0.02)
    ln2_b = nrm((DEPTH, D_MODEL), 0.02)
    return {
        'x_prompt': x_prompt, 'x_sample': x_sample,
        'cache_attn_k': cache_attn_k, 'cache_attn_v': cache_attn_v,
        'state_gdn_conv': state_gdn_conv, 'state_gdn_rec': state_gdn_rec, 'state_pool': state_pool,
        'w_in': w_in, 'attn_sinks': attn_sinks, 'gdn_conv_w': gdn_conv_w, 'gdn_a_log': gdn_a_log,
        'gdn_dt_bias': gdn_dt_bias, 'gdn_norm_g': gdn_norm_g, 'pool_w': pool_w, 'pool_scale': pool_scale,
        'w_branch_a': w_branch_a, 'w_branch_b': w_branch_b, 'w_branch_c': w_branch_c, 'w_o': w_o,
        'ln1_g': ln1_g, 'ln1_b': ln1_b, 'router_group_w': router_group_w, 'router_group_b': router_group_b,
        'router_expert_w': router_expert_w, 'router_expert_b': router_expert_b,
        'w_gate': w_gate, 'w_up': w_up, 'w_down': w_down, 'ln2_g': ln2_g, 'ln2_b': ln2_b,
    }


def reference(x_prompt, x_sample, cache_attn_k, cache_attn_v, state_gdn_conv, state_gdn_rec, state_pool,
              w_in, attn_sinks, gdn_conv_w, gdn_a_log, gdn_dt_bias, gdn_norm_g, pool_w, pool_scale,
              w_branch_a, w_branch_b, w_branch_c, w_o, ln1_g, ln1_b, router_group_w, router_group_b,
              router_expert_w, router_expert_b, w_gate, w_up, w_down, ln2_g, ln2_b):
    hp, hs = x_prompt, x_sample
    bp = x_prompt.shape[0]
    dt = x_prompt.dtype
    sp = [[] for _ in range(5)]
    ss = [[] for _ in range(5)]
    for l in range(DEPTH):
        lw = {
            'w_in': w_in[l], 'attn_sinks': attn_sinks[l], 'gdn_conv_w': gdn_conv_w[l],
            'gdn_a_log': gdn_a_log[l], 'gdn_dt_bias': gdn_dt_bias[l], 'gdn_norm_g': gdn_norm_g[l],
            'pool_w': pool_w[l], 'pool_scale': pool_scale[l], 'w_branch_a': w_branch_a[l],
            'w_branch_b': w_branch_b[l], 'w_branch_c': w_branch_c[l], 'w_o': w_o[l],
            'ln1_g': ln1_g[l], 'ln1_b': ln1_b[l], 'router_group_w': router_group_w[l],
            'router_group_b': router_group_b[l], 'router_expert_w': router_expert_w[l],
            'router_expert_b': router_expert_b[l], 'w_gate': w_gate[l], 'w_up': w_up[l],
            'w_down': w_down[l], 'ln2_g': ln2_g[l], 'ln2_b': ln2_b[l],
        }
        zkv = jnp.zeros((bp, WINDOW, A_KV_HEADS, A_HEAD_DIM), dt)
        zconv = jnp.zeros((bp, CONV_W - 1, 3 * B_WIDTH), dt)
        zrec = jnp.zeros((bp, B_HEADS, B_HEAD_DIM, B_HEAD_DIM), F32)
        zpool = jnp.zeros((bp, POOL_HIST, C_WIDTH), dt)
        hp, *st_p = trunk_layer(hp, zkv, zkv, False, zconv, zrec, zpool, 0, lw)
        hs, *st_s = trunk_layer(hs, cache_attn_k[l], cache_attn_v[l], True, state_gdn_conv[l],
                                state_gdn_rec[l], state_pool[l], PAST_LEN, lw)
        for i in range(5):
            sp[i].append(st_p[i])
            ss[i].append(st_s[i])
    return (hp, hs,
            jnp.stack(sp[0]), jnp.stack(sp[1]), jnp.stack(sp[2]), jnp.stack(sp[3]), jnp.stack(sp[4]),
            jnp.stack(ss[0]), jnp.stack(ss[1]), jnp.stack(ss[2]), jnp.stack(ss[3]), jnp.stack(ss[4]))
```

```python
import functools

import jax
import jax.numpy as jnp
import numpy as np
from jax import lax
from jax.experimental import pallas as pl
from jax.experimental.pallas import tpu as pltpu

F32 = jnp.float32
BF16 = jnp.bfloat16

D_MODEL = 1024
BATCH = 2
SEQ = 8192
DEPTH = 2
DEC_BATCH = 128
PAST_LEN = 16384
WINDOW = 128
A_HEADS = 8
A_KV_HEADS = 2
A_HEAD_DIM = 64
A_GROUP = A_HEADS // A_KV_HEADS
A_WIDTH = A_HEADS * A_HEAD_DIM
B_HEADS = 4
B_HEAD_DIM = 128
B_WIDTH = B_HEADS * B_HEAD_DIM
CONV_W = 4
GDN_CHUNK = 64
POOL_WINDOWS = (2, 4, 8, 16)
POOL_GROUPS = 4
C_WIDTH = 512
POOL_GROUP_DIM = C_WIDTH // POOL_GROUPS
POOL_HIST = max(POOL_WINDOWS) - 1
N_BRANCHES = 3
IN_SPLIT_SIZES = (A_WIDTH, A_KV_HEADS * A_HEAD_DIM, A_KV_HEADS * A_HEAD_DIM, 3 * B_WIDTH,
                  B_WIDTH, B_HEADS, B_HEADS, C_WIDTH, N_BRANCHES * D_MODEL)
N_GROUPS = 4
EXPERTS_PER_GROUP = 8
N_EXPERTS = N_GROUPS * EXPERTS_PER_GROUP
TOP_K = 2
EXPERT_FF = 512
DN_ALPHA = (2.0 * DEPTH) ** 0.25
LN_EPS = 1e-5
RMS_EPS = 1e-6
NEG_INF = -1e30

LANES = 128
SUBLANES = 8
ROW_TILES = D_MODEL // LANES
VMEM_LIMIT = 56 * 1024 * 1024

N_PROMPT = BATCH * SEQ
N_ROWS = N_PROMPT + DEC_BATCH

SEG_GATES = (0, N_BRANCHES * D_MODEL)
SEG_BQKV = (3072, 3 * B_WIDTH)
SEG_AQ = (4608, A_WIDTH)
SEG_Z = (5120, B_WIDTH)
SEG_CP = (5632, C_WIDTH)
SEG_KV = (6144, 2 * A_KV_HEADS * A_HEAD_DIM)
SEG_AB = (6400, LANES)
IN_PACKED = 6528

MOE_ROWS = 256
MOE_ASSIGN = N_ROWS * TOP_K
MOE_CAP = -(-MOE_ASSIGN // MOE_ROWS) * MOE_ROWS + N_EXPERTS * MOE_ROWS
MOE_BLOCKS = MOE_CAP // MOE_ROWS
ROUTER_COLS = N_GROUPS + N_EXPERTS


def _sigmoid(x):
    return 1.0 / (1.0 + jnp.exp(-x))


def _silu(x):
    return x * _sigmoid(x)


def _dot(a, b):
    return jnp.dot(a, b, preferred_element_type=F32)


def _dot_nt(a, b):
    return lax.dot_general(a, b, (((1,), (1,)), ((), ())), preferred_element_type=F32)


def _dot_tn(a, b):
    return lax.dot_general(a, b, (((0,), (0,)), ((), ())), preferred_element_type=F32)


def _split3(x):
    hi = x.astype(BF16)
    r = x - hi.astype(F32)
    mid = r.astype(BF16)
    lo = (r - mid.astype(F32)).astype(BF16)
    return hi, mid, lo


def _params(sem=None):
    return pltpu.CompilerParams(dimension_semantics=sem, vmem_limit_bytes=VMEM_LIMIT)


def _const_spec(shape):
    nd = len(shape)
    return pl.BlockSpec(shape, lambda *_: (0,) * nd)


def _pack_w_in(w):
    aq, ak, av, bqkv, bz, ba, bb, cp, gate = jnp.split(w, np.cumsum(IN_SPLIT_SIZES)[:-1].tolist(), axis=-1)
    aq = aq * (A_HEAD_DIM ** -0.5)
    ab = jnp.concatenate([ba, bb, jnp.zeros((D_MODEL, LANES - 2 * B_HEADS), w.dtype)], -1)
    return jnp.concatenate([gate, bqkv, aq, bz, cp, ak, av, ab], -1).astype(BF16)


def _in_proj_kernel(x_ref, w_ref, gates_ref, bqkv_ref, aq_ref, z_ref, cp_ref, kv_ref, ab_ref):
    x = x_ref[...].astype(BF16)

    def mm(lo, n):
        return _dot(x, w_ref[:, lo:lo + n])

    step = 512
    for c in range(SEG_GATES[1] // step):
        gates_ref[:, c * step:(c + 1) * step] = _sigmoid(mm(SEG_GATES[0] + c * step, step)).astype(BF16)
    for c in range(SEG_BQKV[1] // step):
        bqkv_ref[:, c * step:(c + 1) * step] = mm(SEG_BQKV[0] + c * step, step)
    aq_ref[...] = mm(*SEG_AQ).astype(BF16)
    z_ref[...] = mm(*SEG_Z)
    cp_ref[...] = mm(*SEG_CP)
    kv_ref[...] = mm(*SEG_KV)
    ab_ref[...] = mm(*SEG_AB)


def _in_proj(x, w_packed, tm):
    n = x.shape[0]
    widths = (SEG_GATES[1], SEG_BQKV[1], SEG_AQ[1], SEG_Z[1], SEG_CP[1], SEG_KV[1], SEG_AB[1])
    dtypes = (BF16, F32, BF16, F32, F32, F32, F32)
    return pl.pallas_call(
        _in_proj_kernel,
        grid=(n // tm,),
        in_specs=[pl.BlockSpec((tm, D_MODEL), lambda i: (i, 0)),
                  pl.BlockSpec((D_MODEL, IN_PACKED), lambda i: (0, 0), pipeline_mode=pl.Buffered(1))],
        out_specs=[pl.BlockSpec((tm, w), lambda i: (i, 0)) for w in widths],
        out_shape=[jax.ShapeDtypeStruct((n, w), dt) for w, dt in zip(widths, dtypes)],
        compiler_params=_params(("arbitrary",)),
        name="in_proj",
    )(x, w_packed)


def _alibi_slope(h):
    return 2.0 ** (-8.0 * (h + 1) / A_HEADS)


def _attn_prompt_kernel(sink_ref, q_ref, kvc_ref, kvp_ref, o_ref):
    n = pl.program_id(1)
    kvc = kvc_ref[...]
    kvp = kvp_ref[...]
    t = lax.broadcasted_iota(jnp.int32, (WINDOW, 2 * WINDOW), 0)
    s = lax.broadcasted_iota(jnp.int32, (WINDOW, 2 * WINDOW), 1)
    dist = WINDOW + t - s
    valid = (dist >= 0) & (dist <= WINDOW) & ((s >= WINDOW) | (n > 0))
    distf = dist.astype(F32)
    hd = A_HEAD_DIM
    vo = A_KV_HEADS * hd
    outs = []
    for kh in range(A_KV_HEADS):
        k = jnp.concatenate([kvp[:, kh * hd:(kh + 1) * hd], kvc[:, kh * hd:(kh + 1) * hd]], axis=0).astype(BF16)
        v = jnp.concatenate([kvp[:, vo + kh * hd:vo + (kh + 1) * hd],
                             kvc[:, vo + kh * hd:vo + (kh + 1) * hd]], axis=0).astype(BF16)
        for g in range(A_GROUP):
            h = kh * A_GROUP + g
            sc = _dot_nt(q_ref[:, h * hd:(h + 1) * hd], k)
            sc = jnp.where(valid, sc - _alibi_slope(h) * distf, NEG_INF)
            sink = sink_ref[h]
            m = jnp.maximum(jnp.max(sc, axis=-1, keepdims=True), sink)
            p = jnp.exp(sc - m)
            denom = jnp.sum(p, axis=-1, keepdims=True) + jnp.exp(sink - m)
            outs.append(_dot(p.astype(BF16), v) / denom)
    o_ref[...] = jnp.concatenate(outs, axis=-1).astype(BF16)


def _attn_prompt(sinks, aq, kv):
    nb = SEQ // WINDOW
    return pl.pallas_call(
        _attn_prompt_kernel,
        grid=(BATCH, nb),
        in_specs=[pl.BlockSpec(memory_space=pltpu.SMEM),
                  pl.BlockSpec((WINDOW, A_WIDTH), lambda b, n: (b * nb + n, 0)),
                  pl.BlockSpec((WINDOW, SEG_KV[1]), lambda b, n: (b * nb + n, 0)),
                  pl.BlockSpec((WINDOW, SEG_KV[1]), lambda b, n: (b * nb + jnp.maximum(n - 1, 0), 0))],
        out_specs=pl.BlockSpec((WINDOW, A_WIDTH), lambda b, n: (b * nb + n, 0)),
        out_shape=jax.ShapeDtypeStruct((N_PROMPT, A_WIDTH), BF16),
        compiler_params=_params(("arbitrary", "arbitrary")),
        name="attn_prompt",
    )(sinks, aq, kv, kv)


ATTN_S_BLOCK = 8


def _attn_sample_kernel(sink_ref, q_ref, kn_ref, vn_ref, ck_ref, cv_ref, o_ref, nk_ref, nv_ref):
    hd = A_HEAD_DIM
    row = lax.broadcasted_iota(jnp.int32, (A_HEADS, 1), 0)
    slope = jnp.zeros((A_HEADS, 1), F32)
    sink = jnp.zeros((A_HEADS, 1), F32)
    for h in range(A_HEADS):
        slope = jnp.where(row == h, _alibi_slope(h), slope)
        sink = jnp.where(row == h, sink_ref[h], sink)
    pos = lax.broadcasted_iota(jnp.int32, (A_HEADS, WINDOW), 1)
    bias = -slope * (WINDOW - pos).astype(F32)
    first = row < A_GROUP
    for b in range(ATTN_S_BLOCK):
        kc = ck_ref[b]
        vc = cv_ref[b]
        qf = q_ref[b]
        q = qf.astype(BF16)
        kn = kn_ref[b:b + 1, :]
        vn = vn_ref[b:b + 1, :]
        s0 = _dot_nt(q, kc[:, 0:hd].astype(BF16))
        s1 = _dot_nt(q, kc[:, hd:2 * hd].astype(BF16))
        sc = jnp.where(first, s0, s1) + bias
        sn0 = jnp.sum(qf * kn[:, 0:hd], axis=-1, keepdims=True)
        sn1 = jnp.sum(qf * kn[:, hd:2 * hd], axis=-1, keepdims=True)
        sn = jnp.where(first, sn0, sn1)
        m = jnp.maximum(jnp.maximum(jnp.max(sc, axis=-1, keepdims=True), sn), sink)
        p = jnp.exp(sc - m)
        pn = jnp.exp(sn - m)
        denom = jnp.sum(p, axis=-1, keepdims=True) + pn + jnp.exp(sink - m)
        pb = p.astype(BF16)
        o0 = _dot(pb, vc[:, 0:hd].astype(BF16)) + pn * vn[:, 0:hd]
        o1 = _dot(pb, vc[:, hd:2 * hd].astype(BF16)) + pn * vn[:, hd:2 * hd]
        o_ref[b] = jnp.where(first, o0, o1) / denom
        nk_ref[b, 0:WINDOW - 1, :] = kc[1:WINDOW, :]
        nk_ref[b, WINDOW - 1:WINDOW, :] = kn
        nv_ref[b, 0:WINDOW - 1, :] = vc[1:WINDOW, :]
        nv_ref[b, WINDOW - 1:WINDOW, :] = vn


def _attn_sample(sinks, q3, kn, vn, ck, cv):
    bs = ATTN_S_BLOCK
    kvw = A_KV_HEADS * A_HEAD_DIM
    return pl.pallas_call(
        _attn_sample_kernel,
        grid=(DEC_BATCH // bs,),
        in_specs=[pl.BlockSpec(memory_space=pltpu.SMEM),
                  pl.BlockSpec((bs, A_HEADS, A_HEAD_DIM), lambda i: (i, 0, 0)),
                  pl.BlockSpec((bs, kvw), lambda i: (i, 0)),
                  pl.BlockSpec((bs, kvw), lambda i: (i, 0)),
                  pl.BlockSpec((bs, WINDOW, kvw), lambda i: (i, 0, 0)),
                  pl.BlockSpec((bs, WINDOW, kvw), lambda i: (i, 0, 0))],
        out_specs=[pl.BlockSpec((bs, A_HEADS, A_HEAD_DIM), lambda i: (i, 0, 0)),
                   pl.BlockSpec((bs, WINDOW, kvw), lambda i: (i, 0, 0)),
                   pl.BlockSpec((bs, WINDOW, kvw), lambda i: (i, 0, 0))],
        out_shape=[jax.ShapeDtypeStruct((DEC_BATCH, A_HEADS, A_HEAD_DIM), F32),
                   jax.ShapeDtypeStruct((DEC_BATCH, WINDOW, kvw), F32),
                   jax.ShapeDtypeStruct((DEC_BATCH, WINDOW, kvw), F32)],
        compiler_params=_params(("arbitrary",)),
        name="attn_sample",
    )(sinks, q3, kn, vn, ck, cv)


GDN_TILE = 256
GDN_STACK = B_HEADS * GDN_CHUNK


def _gdn_gates(ab, alog, dtb):
    xs = ab + dtb
    softplus = jnp.maximum(xs, 0.0) + jnp.log(1.0 + jnp.exp(-jnp.abs(xs)))
    g = -jnp.exp(alog) * softplus
    beta = _sigmoid(ab)
    return g, beta


def _l2norm(x):
    return x * lax.rsqrt(jnp.sum(x * x, axis=-1, keepdims=True) + RMS_EPS)


def _gdn_out_norm(o, z, norm_g):
    return o * lax.rsqrt(jnp.mean(o * o, axis=-1, keepdims=True) + RMS_EPS) * norm_g * _silu(z)


def _gdn_prompt_kernel(x_ref, z_ref, ab_ref, cw_ref, alog_ref, dtb_ref, ng_ref,
                       o_ref, sfin_ref, xbuf_ref, s_ref):
    n = pl.program_id(1)
    t_rows = GDN_TILE
    c_rows = GDN_CHUNK
    hd = B_HEAD_DIM
    pad = SUBLANES

    @pl.when(n == 0)
    def _():
        xbuf_ref[0:pad, :] = jnp.zeros((pad, 3 * B_WIDTH), F32)
        s_ref[...] = jnp.zeros_like(s_ref)

    x = x_ref[...]
    xbuf_ref[pad:pad + t_rows, :] = x
    cw = cw_ref[...]
    conv = x * cw[CONV_W - 1:CONV_W, :]
    for j in range(CONV_W - 1):
        back = CONV_W - 1 - j
        conv = conv + xbuf_ref[pad - back:pad - back + t_rows, :] * cw[j:j + 1, :]
    xbuf_ref[0:pad, :] = x[t_rows - pad:t_rows, :]
    conv = _silu(conv)

    g_all, beta_all = _gdn_gates(ab_ref[...], alog_ref[...], dtb_ref[...])
    ri = lax.broadcasted_iota(jnp.int32, (t_rows, t_rows), 0)
    ci = lax.broadcasted_iota(jnp.int32, (t_rows, t_rows), 1)
    tri = jnp.where((ri // c_rows == ci // c_rows) & (ri >= ci), 1.0, 0.0).astype(BF16)
    g_hi, g_mid, g_lo = _split3(g_all)
    gcum_all = _dot(tri, g_hi) + _dot(tri, g_mid) + _dot(tri, g_lo)

    qn, kn, vv = [], [], []
    for h in range(B_HEADS):
        qn.append(_l2norm(conv[:, h * hd:(h + 1) * hd]) * (hd ** -0.5))
        kn.append(_l2norm(conv[:, B_WIDTH + h * hd:B_WIDTH + (h + 1) * hd]))
        vv.append(conv[:, 2 * B_WIDTH + h * hd:2 * B_WIDTH + (h + 1) * hd])

    st = GDN_STACK
    r = lax.broadcasted_iota(jnp.int32, (st, st), 0)
    c = lax.broadcasted_iota(jnp.int32, (st, st), 1)
    same = (r // c_rows) == (c // c_rows)
    incl = same & (r >= c)
    strict = same & (r > c)
    eye = (r == c)

    def stack(parts, lo):
        return jnp.concatenate([p[lo:lo + c_rows] for p in parts], axis=0)

    o_chunks = []
    for ck in range(t_rows // c_rows):
        lo = ck * c_rows
        qs, ks, vs = stack(qn, lo), stack(kn, lo), stack(vv, lo)
        gcol = jnp.concatenate([gcum_all[lo:lo + c_rows, h:h + 1] for h in range(B_HEADS)], axis=0)
        bcol = jnp.concatenate([beta_all[lo:lo + c_rows, B_HEADS + h:B_HEADS + h + 1] for h in range(B_HEADS)], axis=0)
        glast = [gcum_all[lo + c_rows - 1:lo + c_rows, h:h + 1] for h in range(B_HEADS)]
        glast_col = jnp.concatenate([jnp.broadcast_to(gl, (c_rows, 1)) for gl in glast], axis=0)
        e_mat = jnp.broadcast_to(gcol, (st, st))
        diff = e_mat - e_mat.T
        decay = jnp.where(incl, jnp.exp(jnp.where(incl, diff, 0.0)), 0.0)
        exp_g = jnp.exp(gcol)
        kb = ks * bcol
        vb = vs * bcol
        ks16 = ks.astype(BF16)
        kk = _dot_nt(kb.astype(BF16), ks16)
        qk = _dot_nt(qs.astype(BF16), ks16)
        m_mat = jnp.where(strict, kk * decay, 0.0)
        p_inv = None
        blk = 1
        while blk < c_rows:
            lvl = ((r // (2 * blk)) == (c // (2 * blk))) & ((r // blk) != (c // blk)) & (r > c)
            a_off = jnp.where(lvl, m_mat, 0.0)
            if p_inv is None:
                p_inv = jnp.where(eye, 1.0, 0.0) - a_off
            else:
                p16 = p_inv.astype(BF16)
                p_inv = p_inv - _dot(_dot(p16, a_off.astype(BF16)).astype(BF16), p16)
            blk *= 2
        uw = _dot(p_inv.astype(BF16), jnp.concatenate([vb, kb * exp_g], axis=1).astype(BF16))
        u = uw[:, 0:hd]
        w = uw[:, hd:2 * hd]
        aqk = qk * decay
        qg = qs * exp_g
        kd = ks * jnp.exp(glast_col - gcol)
        vnew, qs_s = [], []
        for h in range(B_HEADS):
            hs = slice(h * c_rows, (h + 1) * c_rows)
            wq = jnp.concatenate([w[hs], qg[hs]], axis=0).astype(BF16)
            rs = _dot(wq, s_ref[h].astype(BF16))
            vnew.append(u[hs] - rs[0:c_rows])
            qs_s.append(rs[c_rows:2 * c_rows])
        vnew_st = jnp.concatenate(vnew, axis=0)
        o_st = jnp.concatenate(qs_s, axis=0) + _dot(aqk.astype(BF16), vnew_st.astype(BF16))
        for h in range(B_HEADS):
            hs = slice(h * c_rows, (h + 1) * c_rows)
            s_ref[h] = s_ref[h] * jnp.exp(glast[h]) + _dot_tn(kd[hs].astype(BF16), vnew[h].astype(BF16))
        o_chunks.append([o_st[h * c_rows:(h + 1) * c_rows] for h in range(B_HEADS)])

    ng = ng_ref[...]
    z = z_ref[...]
    outs = []
    for h in range(B_HEADS):
        o_h = jnp.concatenate([oc[h] for oc in o_chunks], axis=0)
        outs.append(_gdn_out_norm(o_h, z[:, h * hd:(h + 1) * hd], ng))
    o_ref[...] = jnp.concatenate(outs, axis=-1).astype(BF16)
    sfin_ref[0] = s_ref[...]


def _gdn_prompt(bqkv, z, ab, conv_w, alog_row, dtb_row, norm_g):
    nt = SEQ // GDN_TILE
    t = GDN_TILE
    return pl.pallas_call(
        _gdn_prompt_kernel,
        grid=(BATCH, nt),
        in_specs=[pl.BlockSpec((t, 3 * B_WIDTH), lambda b, n: (b * nt + n, 0)),
                  pl.BlockSpec((t, B_WIDTH), lambda b, n: (b * nt + n, 0)),
                  pl.BlockSpec((t, LANES), lambda b, n: (b * nt + n, 0)),
                  _const_spec((CONV_W, 3 * B_WIDTH)),
                  _const_spec((1, LANES)),
                  _const_spec((1, LANES)),
                  _const_spec((1, B_HEAD_DIM))],
        out_specs=[pl.BlockSpec((t, B_WIDTH), lambda b, n: (b * nt + n, 0)),
                   pl.BlockSpec((1, B_HEADS, B_HEAD_DIM, B_HEAD_DIM), lambda b, n: (b, 0, 0, 0))],
        out_shape=[jax.ShapeDtypeStruct((N_PROMPT, B_WIDTH), BF16),
                   jax.ShapeDtypeStruct((BATCH, B_HEADS, B_HEAD_DIM, B_HEAD_DIM), F32)],
        scratch_shapes=[pltpu.VMEM((SUBLANES + t, 3 * B_WIDTH), F32),
                        pltpu.VMEM((B_HEADS, B_HEAD_DIM, B_HEAD_DIM), F32)],
        compiler_params=_params(("arbitrary", "arbitrary")),
        name="gdn_prompt",
    )(bqkv, z, ab, conv_w, alog_row, dtb_row, norm_g)


GDN_S_BLOCK = 8


def _gdn_sample_kernel(st_ref, x_ref, z_ref, ab_ref, cw_ref, alog_ref, dtb_ref, ng_ref, s_ref,
                       o_ref, nst_ref, ns_ref):
    hd = B_HEAD_DIM
    x = x_ref[...]
    cw = cw_ref[...]
    conv = x * cw[CONV_W - 1:CONV_W, :]
    for j in range(CONV_W - 1):
        conv = conv + st_ref[j] * cw[j:j + 1, :]
    for j in range(CONV_W - 2):
        nst_ref[j] = st_ref[j + 1]
    nst_ref[CONV_W - 2] = x
    conv = _silu(conv)
    g_all, beta_all = _gdn_gates(ab_ref[...], alog_ref[...], dtb_ref[...])
    eg_all = jnp.exp(g_all)
    r = lax.broadcasted_iota(jnp.int32, (hd, hd), 0)
    c = lax.broadcasted_iota(jnp.int32, (hd, hd), 1)
    eye = r == c
    zpad = jnp.zeros((SUBLANES - 2, hd), F32)
    ng = ng_ref[...]
    z = z_ref[...]
    outs = []
    for h in range(B_HEADS):
        q = _l2norm(conv[:, h * hd:(h + 1) * hd]) * (hd ** -0.5)
        k = _l2norm(conv[:, B_WIDTH + h * hd:B_WIDTH + (h + 1) * hd])
        v = conv[:, 2 * B_WIDTH + h * hd:2 * B_WIDTH + (h + 1) * hd]
        beta = beta_all[:, B_HEADS + h:B_HEADS + h + 1]
        eg = eg_all[:, h:h + 1]
        qk = jnp.sum(q * k, axis=-1, keepdims=True)
        rows = []
        for b in range(GDN_S_BLOCK):
            kb = k[b:b + 1]
            s_bh = s_ref[b, h]
            kq = jnp.concatenate([kb, q[b:b + 1], zpad], axis=0).astype(BF16)
            rs = _dot(kq, s_bh.astype(BF16))
            bb = beta[b:b + 1]
            egb = eg[b:b + 1]
            v_new = bb * v[b:b + 1] - (bb * egb) * rs[0:1]
            rows.append(egb * rs[1:2] + qk[b:b + 1] * v_new)
            kdiag = jnp.where(eye, jnp.broadcast_to(kb, (hd, hd)), 0.0).astype(BF16)
            outer = _dot(kdiag, jnp.broadcast_to(v_new, (hd, hd)).astype(BF16))
            ns_ref[b, h] = s_bh * egb + outer
        o_h = jnp.concatenate(rows, axis=0)
        outs.append(_gdn_out_norm(o_h, z[:, h * hd:(h + 1) * hd], ng))
    o_ref[...] = jnp.concatenate(outs, axis=-1).astype(BF16)


def _gdn_sample(conv_st, bqkv, z, ab, conv_w, alog_row, dtb_row, norm_g, s0):
    bs = GDN_S_BLOCK
    cw3 = 3 * B_WIDTH
    return pl.pallas_call(
        _gdn_sample_kernel,
        grid=(DEC_BATCH // bs,),
        in_specs=[pl.BlockSpec((CONV_W - 1, bs, cw3), lambda i: (0, i, 0)),
                  pl.BlockSpec((bs, cw3), lambda i: (i, 0)),
                  pl.BlockSpec((bs, B_WIDTH), lambda i: (i, 0)),
                  pl.BlockSpec((bs, LANES), lambda i: (i, 0)),
                  _const_spec((CONV_W, cw3)),
                  _const_spec((1, LANES)),
                  _const_spec((1, LANES)),
                  _const_spec((1, B_HEAD_DIM)),
                  pl.BlockSpec((bs, B_HEADS, B_HEAD_DIM, B_HEAD_DIM), lambda i: (i, 0, 0, 0))],
        out_specs=[pl.BlockSpec((bs, B_WIDTH), lambda i: (i, 0)),
                   pl.BlockSpec((CONV_W - 1, bs, cw3), lambda i: (0, i, 0)),
                   pl.BlockSpec((bs, B_HEADS, B_HEAD_DIM, B_HEAD_DIM), lambda i: (i, 0, 0, 0))],
        out_shape=[jax.ShapeDtypeStruct((DEC_BATCH, B_WIDTH), BF16),
                   jax.ShapeDtypeStruct((CONV_W - 1, DEC_BATCH, cw3), F32),
                   jax.ShapeDtypeStruct((DEC_BATCH, B_HEADS, B_HEAD_DIM, B_HEAD_DIM), F32)],
        compiler_params=_params(("arbitrary",)),
        name="gdn_sample",
    )(conv_st, bqkv, z, ab, conv_w, alog_row, dtb_row, norm_g, s0)


POOL_TILE = 512
POOL_PAD = 16


def _pool_project(d, pw_ref, scale):
    outs = []
    for gi in range(POOL_GROUPS):
        lo = gi * POOL_GROUP_DIM
        outs.append(_dot(d[:, lo:lo + POOL_GROUP_DIM].astype(BF16), pw_ref[gi].astype(BF16)))
    return jnp.concatenate(outs, axis=-1) * scale


def _pool_prompt_kernel(x_ref, pw_ref, ps_ref, o_ref, buf_ref):
    n = pl.program_id(1)
    t = POOL_TILE
    pad = POOL_PAD
    gd = POOL_GROUP_DIM

    @pl.when(n == 0)
    def _():
        buf_ref[0:pad, :] = jnp.zeros((pad, C_WIDTH), F32)

    x = x_ref[...]
    buf_ref[pad:pad + t, :] = x
    s1 = buf_ref[1:pad + t, :]
    s2 = s1 + buf_ref[0:pad + t - 1, :]
    s4 = s2[2:, gd:] + s2[:-2, gd:]
    s8 = s4[4:, gd:] + s4[:-4, gd:]
    s16 = s8[8:, gd:] + s8[:-8, gd:]
    sums = (s2[15:, 0:gd], s4[13:, 0:gd], s8[9:, 0:gd], s16[1:, :])
    pos = n * t + lax.broadcasted_iota(jnp.int32, (t, 1), 0) + 1
    means = []
    for gi, w in enumerate(POOL_WINDOWS):
        cnt = jnp.minimum(pos, w).astype(F32)
        means.append(sums[gi] / cnt)
    d = jnp.concatenate(means, axis=-1) - x
    o_ref[...] = _pool_project(d, pw_ref, ps_ref[...]).astype(BF16)
    buf_ref[0:pad, :] = x[t - pad:t, :]


def _pool_prompt(cp, pool_w, pool_scale):
    nt = SEQ // POOL_TILE
    t = POOL_TILE
    return pl.pallas_call(
        _pool_prompt_kernel,
        grid=(BATCH, nt),
        in_specs=[pl.BlockSpec((t, C_WIDTH), lambda b, n: (b * nt + n, 0)),
                  _const_spec((POOL_GROUPS, POOL_GROUP_DIM, POOL_GROUP_DIM)),
                  _const_spec((1, C_WIDTH))],
        out_specs=pl.BlockSpec((t, C_WIDTH), lambda b, n: (b * nt + n, 0)),
        out_shape=jax.ShapeDtypeStruct((N_PROMPT, C_WIDTH), BF16),
        scratch_shapes=[pltpu.VMEM((POOL_PAD + t, C_WIDTH), F32)],
        compiler_params=_params(("arbitrary", "arbitrary")),
        name="pool_prompt",
    )(cp, pool_w, pool_scale)


def _pool_sample_kernel(st_ref, x_ref, pw_ref, ps_ref, o_ref, nst_ref):
    x = x_ref[...]
    gd = POOL_GROUP_DIM
    means = []
    for gi, w in enumerate(POOL_WINDOWS):
        lo = gi * gd
        acc = x[:, lo:lo + gd]
        for j in range(1, w):
            acc = acc + st_ref[POOL_HIST - j][:, lo:lo + gd]
        means.append(acc / float(min(PAST_LEN + 1, w)))
    d = jnp.concatenate(means, axis=-1) - x
    o_ref[...] = _pool_project(d, pw_ref, ps_ref[...]).astype(BF16)
    for j in range(POOL_HIST - 1):
        nst_ref[j] = st_ref[j + 1]
    nst_ref[POOL_HIST - 1] = x


def _pool_sample(pool_st, cp, pool_w, pool_scale):
    return pl.pallas_call(
        _pool_sample_kernel,
        out_shape=[jax.ShapeDtypeStruct((DEC_BATCH, C_WIDTH), BF16),
                   jax.ShapeDtypeStruct((POOL_HIST, DEC_BATCH, C_WIDTH), F32)],
        compiler_params=_params(),
        name="pool_sample",
    )(pool_st, cp, pool_w, pool_scale)


def _layer_norm(y, g, b):
    mu = jnp.mean(y, axis=-1, keepdims=True)
    yc = y - mu
    var = jnp.mean(yc * yc, axis=-1, keepdims=True)
    return yc * lax.rsqrt(var + LN_EPS) * g + b


def _store_token_tiled(ref, val, rows):
    for s in range(ROW_TILES):
        ref[pl.ds(s, rows, stride=ROW_TILES), :] = val[:, s * LANES:(s + 1) * LANES]


def _load_token_tiled(ref, base, rows):
    return jnp.concatenate([ref[pl.ds(base + s, rows, stride=ROW_TILES), :] for s in range(ROW_TILES)], axis=-1)


def _merge_kernel(hp_ref, oap_ref, obp_ref, ocp_ref, gp_ref, hs_ref, oas_ref, obs_ref, ocs_ref, gs_ref,
                  wa_ref, wb_ref, wc_ref, wo_ref, lng_ref, lnb_ref, wr_ref, br_ref,
                  h1_ref, h1t_ref, lg_ref, *, n_prompt_tiles):
    i = pl.program_id(0)

    def run(h_ref, oa_ref, ob_ref, oc_ref, g_ref):
        rows = h_ref.shape[0]
        merged = (g_ref[:, 0:D_MODEL].astype(F32) * _dot(oa_ref[...], wa_ref[...])
                  + g_ref[:, D_MODEL:2 * D_MODEL].astype(F32) * _dot(ob_ref[...], wb_ref[...])
                  + g_ref[:, 2 * D_MODEL:3 * D_MODEL].astype(F32) * _dot(oc_ref[...], wc_ref[...]))
        mix = _dot(merged.astype(BF16), wo_ref[...])
        h1 = _layer_norm(DN_ALPHA * h_ref[...] + mix, lng_ref[...], lnb_ref[...])
        h1_ref[0:rows, :] = h1
        _store_token_tiled(h1t_ref, h1, rows)
        hh, hm, _ = _split3(h1)
        wh, wm, _ = _split3(wr_ref[...])
        lg_ref[0:rows, :] = _dot(hh, wh) + _dot(hh, wm) + _dot(hm, wh) + br_ref[...]

    @pl.when(i < n_prompt_tiles)
    def _():
        run(hp_ref, oap_ref, obp_ref, ocp_ref, gp_ref)

    @pl.when(i == n_prompt_tiles)
    def _():
        run(hs_ref, oas_ref, obs_ref, ocs_ref, gs_ref)


def _merge(prompt, sample, wa, wb, wc, wo, lng, lnb, wr, br, tm):
    n_p = N_PROMPT // tm
    last = n_p - 1
    widths = (D_MODEL, A_WIDTH, B_WIDTH, C_WIDTH, N_BRANCHES * D_MODEL)
    in_specs = ([pl.BlockSpec((tm, w), lambda i: (jnp.minimum(i, last), 0)) for w in widths]
                + [_const_spec((DEC_BATCH, w)) for w in widths]
                + [_const_spec((A_WIDTH, D_MODEL)), _const_spec((B_WIDTH, D_MODEL)), _const_spec((C_WIDTH, D_MODEL)),
                   _const_spec((D_MODEL, D_MODEL)), _const_spec((1, D_MODEL)), _const_spec((1, D_MODEL)),
                   _const_spec((D_MODEL, LANES)), _const_spec((1, LANES))])
    return pl.pallas_call(
        functools.partial(_merge_kernel, n_prompt_tiles=n_p),
        grid=(n_p + 1,),
        in_specs=in_specs,
        out_specs=[pl.BlockSpec((tm, D_MODEL), lambda i: (i, 0)),
                   pl.BlockSpec((tm * ROW_TILES, LANES), lambda i: (i, 0)),
                   pl.BlockSpec((tm, LANES), lambda i: (i, 0))],
        out_shape=[jax.ShapeDtypeStruct((N_ROWS, D_MODEL), F32),
                   jax.ShapeDtypeStruct((N_ROWS * ROW_TILES, LANES), F32),
                   jax.ShapeDtypeStruct((N_ROWS, LANES), F32)],
        compiler_params=_params(("arbitrary",)),
        name="merge",
    )(*prompt, *sample, wa, wb, wc, wo, lng, lnb, wr, br)


def _route(logits):
    glog = logits[:, :N_GROUPS]
    elog_all = logits[:, N_GROUPS:ROUTER_COLS].reshape(-1, N_GROUPS, EXPERTS_PER_GROUP)
    gsel = jnp.argmax(glog, axis=-1)
    gw = jnp.take_along_axis(jax.nn.softmax(glog, axis=-1), gsel[:, None], 1)[:, 0]
    elog = jnp.take_along_axis(elog_all, gsel[:, None, None], 1)[:, 0]
    topv, topi = lax.top_k(elog, TOP_K)
    wts = gw[:, None] * jax.nn.softmax(topv, axis=-1)
    eid = (gsel[:, None] * EXPERTS_PER_GROUP + topi).reshape(-1).astype(jnp.int32)
    onehot = (eid[:, None] == jnp.arange(N_EXPERTS, dtype=jnp.int32)[None, :]).astype(jnp.int32)
    csum = jnp.cumsum(onehot, axis=0)
    rank = jnp.take_along_axis(csum, eid[:, None], 1)[:, 0] - 1
    sizes = csum[-1]
    padded = (sizes + MOE_ROWS - 1) // MOE_ROWS * MOE_ROWS
    end_padded = jnp.cumsum(padded)
    start_padded = end_padded - padded
    dest = (start_padded[eid] + rank).astype(jnp.int32)
    tok = jnp.repeat(jnp.arange(N_ROWS, dtype=jnp.int32), TOP_K)
    buf_tok = jnp.zeros((MOE_CAP,), jnp.int32).at[dest].set(tok)
    buf_w = jnp.zeros((MOE_CAP,), F32).at[dest].set(wts.reshape(-1))
    blk_start = jnp.arange(MOE_BLOCKS, dtype=jnp.int32) * MOE_ROWS
    blk_exp = jnp.minimum(jnp.searchsorted(end_padded, blk_start, side='right'), N_EXPERTS - 1).astype(jnp.int32)
    n_used = (end_padded[-1] // MOE_ROWS).astype(jnp.int32)
    blk_exp = jnp.where(blk_start < end_padded[-1], blk_exp, blk_exp[jnp.maximum(n_used - 1, 0)])
    return dest, buf_tok, buf_w, blk_exp, n_used.reshape(1)


def _gather_rows(idx_ref, src_ref, dst_ref, sem, count):
    def body(j, carry):
        t = idx_ref[0, 0, j]
        pltpu.make_async_copy(src_ref.at[pl.ds(pl.multiple_of(t * ROW_TILES, ROW_TILES), ROW_TILES), :],
                              dst_ref.at[pl.ds(pl.multiple_of(j * ROW_TILES, ROW_TILES), ROW_TILES), :],
                              sem).start()
        return carry
    lax.fori_loop(0, count, body, 0)


def _gather_wait(src_ref, dst_ref, sem, count):
    pltpu.make_async_copy(src_ref.at[pl.ds(0, count * ROW_TILES), :], dst_ref, sem).wait()


def _expert_kernel(bexp_ref, nused_ref, idx_ref, idxn_ref, w_ref, x_hbm, wg_ref, wu_ref, wd_ref,
                   y_ref, xbuf_ref, sem_ref, wg16_ref, wu16_ref, wd16_ref):
    i = pl.program_id(0)
    n_used = nused_ref[0]
    slot = i % 2
    rows = MOE_ROWS

    @pl.when(i == 0)
    def _():
        _gather_rows(idx_ref, x_hbm, xbuf_ref.at[0], sem_ref.at[0], rows)

    @pl.when(i + 1 < n_used)
    def _():
        _gather_rows(idxn_ref, x_hbm, xbuf_ref.at[1 - slot], sem_ref.at[1 - slot], rows)

    @pl.when((i == 0) | (bexp_ref[i] != bexp_ref[jnp.maximum(i - 1, 0)]))
    def _():
        wg16_ref[...] = wg_ref[0].astype(BF16)
        wu16_ref[...] = wu_ref[0].astype(BF16)
        wd16_ref[...] = wd_ref[0].astype(BF16)

    @pl.when(i < n_used)
    def _():
        _gather_wait(x_hbm, xbuf_ref.at[slot], sem_ref.at[slot], rows)
        x = _load_token_tiled(xbuf_ref.at[slot], 0, rows).astype(BF16)
        hid = _silu(_dot(x, wg16_ref[...])) * _dot(x, wu16_ref[...])
        y = _dot(hid.astype(BF16), wd16_ref[...]) * w_ref[0]
        _store_token_tiled(y_ref, y, rows)

    @pl.when(i >= jnp.maximum(n_used, 1))
    def _():
        y_ref[...] = jnp.zeros_like(y_ref)


def _experts(blk_exp, n_used, buf_tok, buf_w, h1t, w_gate, w_up, w_down):
    rows = MOE_ROWS
    idx3 = buf_tok.reshape(MOE_BLOCKS, 1, rows)
    w3 = buf_w.reshape(MOE_BLOCKS, rows, 1)
    last = MOE_BLOCKS - 1
    grid_spec = pltpu.PrefetchScalarGridSpec(
        num_scalar_prefetch=2,
        grid=(MOE_BLOCKS,),
        in_specs=[pl.BlockSpec((1, 1, rows), lambda i, be, nu: (i, 0, 0), memory_space=pltpu.SMEM),
                  pl.BlockSpec((1, 1, rows), lambda i, be, nu: (jnp.minimum(i + 1, last), 0, 0),
                               memory_space=pltpu.SMEM),
                  pl.BlockSpec((1, rows, 1), lambda i, be, nu: (i, 0, 0)),
                  pl.BlockSpec(memory_space=pl.ANY),
                  pl.BlockSpec((1, D_MODEL, EXPERT_FF), lambda i, be, nu: (be[i], 0, 0)),
                  pl.BlockSpec((1, D_MODEL, EXPERT_FF), lambda i, be, nu: (be[i], 0, 0)),
                  pl.BlockSpec((1, EXPERT_FF, D_MODEL), lambda i, be, nu: (be[i], 0, 0))],
        out_specs=pl.BlockSpec((rows * ROW_TILES, LANES), lambda i, be, nu: (i, 0)),
        scratch_shapes=[pltpu.VMEM((2, rows * ROW_TILES, LANES), F32),
                        pltpu.SemaphoreType.DMA((2,)),
                        pltpu.VMEM((D_MODEL, EXPERT_FF), BF16),
                        pltpu.VMEM((D_MODEL, EXPERT_FF), BF16),
                        pltpu.VMEM((EXPERT_FF, D_MODEL), BF16)])
    return pl.pallas_call(
        _expert_kernel,
        grid_spec=grid_spec,
        out_shape=jax.ShapeDtypeStruct((MOE_CAP * ROW_TILES, LANES), F32),
        compiler_params=_params(("arbitrary",)),
        name="experts",
    )(blk_exp, n_used, idx3, idx3, w3, h1t, w_gate, w_up, w_down)


def _combine_kernel(idx_ref, idxn_ref, h1_ref, y_hbm, lng_ref, lnb_ref, o_ref, ybuf_ref, sem_ref, *, n_tiles):
    i = pl.program_id(0)
    tm = h1_ref.shape[0]
    slot = i % 2
    count = TOP_K * tm

    @pl.when(i == 0)
    def _():
        _gather_rows(idx_ref, y_hbm, ybuf_ref.at[0], sem_ref.at[0], count)

    @pl.when(i + 1 < n_tiles)
    def _():
        _gather_rows(idxn_ref, y_hbm, ybuf_ref.at[1 - slot], sem_ref.at[1 - slot], count)

    _gather_wait(y_hbm, ybuf_ref.at[slot], sem_ref.at[slot], count)
    y = DN_ALPHA * h1_ref[...]
    for k in range(TOP_K):
        y = y + _load_token_tiled(ybuf_ref.at[slot], k * tm * ROW_TILES, tm)
    o_ref[...] = _layer_norm(y, lng_ref[...], lnb_ref[...])


def _combine(dest, h1, yb, lng, lnb, tm, row0, n):
    n_tiles = n // tm
    blk0 = row0 // tm
    idx = dest.reshape(N_ROWS, TOP_K)[row0:row0 + n].reshape(n_tiles, tm, TOP_K)
    idx = jnp.transpose(idx, (0, 2, 1)).reshape(n_tiles, 1, TOP_K * tm)
    last = n_tiles - 1
    return pl.pallas_call(
        functools.partial(_combine_kernel, n_tiles=n_tiles),
        grid=(n_tiles,),
        in_specs=[pl.BlockSpec((1, 1, TOP_K * tm), lambda i: (i, 0, 0), memory_space=pltpu.SMEM),
                  pl.BlockSpec((1, 1, TOP_K * tm), lambda i: (jnp.minimum(i + 1, last), 0, 0),
                               memory_space=pltpu.SMEM),
                  pl.BlockSpec((tm, D_MODEL), lambda i: (blk0 + i, 0)),
                  pl.BlockSpec(memory_space=pl.ANY),
                  _const_spec((1, D_MODEL)), _const_spec((1, D_MODEL))],
        out_specs=pl.BlockSpec((tm, D_MODEL), lambda i: (i, 0)),
        out_shape=jax.ShapeDtypeStruct((n, D_MODEL), F32),
        scratch_shapes=[pltpu.VMEM((2, TOP_K * tm * ROW_TILES, LANES), F32),
                        pltpu.SemaphoreType.DMA((2,))],
        compiler_params=_params(("arbitrary",)),
        name="combine",
    )(idx, idx, h1, yb, lng, lnb)


PROMPT_TILE = 512
COMBINE_TILE = 256


def _row(v, width=None):
    v = v.reshape(1, -1).astype(F32)
    if width is not None and v.shape[1] < width:
        v = jnp.pad(v, ((0, 0), (0, width - v.shape[1])))
    return v


def _layer(hp, hs, ck, cv, conv_st, rec_st, pool_st, lw):
    w_packed = _pack_w_in(lw['w_in'])
    alog_row = _row(lw['gdn_a_log'], LANES)
    dtb_row = _row(lw['gdn_dt_bias'], LANES)
    norm_g = _row(lw['gdn_norm_g'])
    pool_scale = _row(lw['pool_scale'])
    sinks = lw['attn_sinks'].astype(F32)
    kvw = A_KV_HEADS * A_HEAD_DIM

    gates_p, bqkv_p, aq_p, z_p, cp_p, kv_p, ab_p = _in_proj(hp, w_packed, PROMPT_TILE)
    oa_p = _attn_prompt(sinks, aq_p, kv_p)
    ob_p, rec_p = _gdn_prompt(bqkv_p, z_p, ab_p, lw['gdn_conv_w'], alog_row, dtb_row, norm_g)
    oc_p = _pool_prompt(cp_p, lw['pool_w'], pool_scale)
    kv_tail = kv_p.reshape(BATCH, SEQ, 2, A_KV_HEADS, A_HEAD_DIM)[:, SEQ - WINDOW:]
    new_k_p, new_v_p = kv_tail[:, :, 0], kv_tail[:, :, 1]
    new_conv_p = bqkv_p.reshape(BATCH, SEQ, 3 * B_WIDTH)[:, SEQ - (CONV_W - 1):]
    new_pool_p = cp_p.reshape(BATCH, SEQ, C_WIDTH)[:, SEQ - POOL_HIST:]

    gates_s, bqkv_s, aq_s, z_s, cp_s, kv_s, ab_s = _in_proj(hs, w_packed, DEC_BATCH)
    oa_s3, nk_s, nv_s = _attn_sample(sinks, aq_s.astype(F32).reshape(DEC_BATCH, A_HEADS, A_HEAD_DIM),
                                     kv_s[:, :kvw], kv_s[:, kvw:],
                                     ck.reshape(DEC_BATCH, WINDOW, kvw), cv.reshape(DEC_BATCH, WINDOW, kvw))
    oa_s = oa_s3.reshape(DEC_BATCH, A_WIDTH).astype(BF16)
    ob_s, nconv_s, rec_s = _gdn_sample(jnp.transpose(conv_st, (1, 0, 2)), bqkv_s, z_s, ab_s, lw['gdn_conv_w'],
                                       alog_row, dtb_row, norm_g, rec_st)
    oc_s, npool_s = _pool_sample(jnp.transpose(pool_st, (1, 0, 2)), cp_s, lw['pool_w'], pool_scale)
    new_k_s = nk_s.reshape(DEC_BATCH, WINDOW, A_KV_HEADS, A_HEAD_DIM)
    new_v_s = nv_s.reshape(DEC_BATCH, WINDOW, A_KV_HEADS, A_HEAD_DIM)
    new_conv_s = jnp.transpose(nconv_s, (1, 0, 2))
    new_pool_s = jnp.transpose(npool_s, (1, 0, 2))

    wa, wb, wc = (lw[k].astype(BF16) for k in ('w_branch_a', 'w_branch_b', 'w_branch_c'))
    wo = lw['w_o'].astype(BF16)
    wr = jnp.concatenate([lw['router_group_w'],
                          jnp.transpose(lw['router_expert_w'], (1, 0, 2)).reshape(D_MODEL, N_EXPERTS),
                          jnp.zeros((D_MODEL, LANES - ROUTER_COLS), F32)], axis=-1)
    br = _row(jnp.concatenate([lw['router_group_b'], lw['router_expert_b'].reshape(-1)]), LANES)
    ln1g, ln1b, ln2g, ln2b = (_row(lw[k]) for k in ('ln1_g', 'ln1_b', 'ln2_g', 'ln2_b'))
    h1, h1t, logits = _merge((hp, oa_p, ob_p, oc_p, gates_p), (hs, oa_s, ob_s, oc_s, gates_s),
                             wa, wb, wc, wo, ln1g, ln1b, wr, br, PROMPT_TILE)
    dest, buf_tok, buf_w, blk_exp, n_used = _route(logits)
    yb = _experts(blk_exp, n_used, buf_tok, buf_w, h1t, lw['w_gate'], lw['w_up'], lw['w_down'])
    h2_p = _combine(dest, h1, yb, ln2g, ln2b, COMBINE_TILE, 0, N_PROMPT)
    h2_s = _combine(dest, h1, yb, ln2g, ln2b, DEC_BATCH, N_PROMPT, DEC_BATCH)
    return (h2_p, h2_s, (new_k_p, new_v_p, new_conv_p, rec_p, new_pool_p),
            (new_k_s, new_v_s, new_conv_s, rec_s, new_pool_s))


def kernel(x_prompt, x_sample, cache_attn_k, cache_attn_v, state_gdn_conv, state_gdn_rec, state_pool, w_in, attn_sinks, gdn_conv_w, gdn_a_log, gdn_dt_bias, gdn_norm_g, pool_w, pool_scale, w_branch_a, w_branch_b, w_branch_c, w_o, ln1_g, ln1_b, router_group_w, router_group_b, router_expert_w, router_expert_b, w_gate, w_up, w_down, ln2_g, ln2_b):
    weights = dict(w_in=w_in, attn_sinks=attn_sinks, gdn_conv_w=gdn_conv_w, gdn_a_log=gdn_a_log,
                   gdn_dt_bias=gdn_dt_bias, gdn_norm_g=gdn_norm_g, pool_w=pool_w, pool_scale=pool_scale,
                   w_branch_a=w_branch_a, w_branch_b=w_branch_b, w_branch_c=w_branch_c, w_o=w_o,
                   ln1_g=ln1_g, ln1_b=ln1_b, router_group_w=router_group_w, router_group_b=router_group_b,
                   router_expert_w=router_expert_w, router_expert_b=router_expert_b,
                   w_gate=w_gate, w_up=w_up, w_down=w_down, ln2_g=ln2_g, ln2_b=ln2_b)
    hp = x_prompt.reshape(N_PROMPT, D_MODEL)
    hs = x_sample.reshape(DEC_BATCH, D_MODEL)
    st_p = [[] for _ in range(5)]
    st_s = [[] for _ in range(5)]
    for l in range(DEPTH):
        lw = {k: v[l] for k, v in weights.items()}
        hp, hs, sp, ss = _layer(hp, hs, cache_attn_k[l], cache_attn_v[l], state_gdn_conv[l],
                                state_gdn_rec[l], state_pool[l], lw)
        for j in range(5):
            st_p[j].append(sp[j])
            st_s[j].append(ss[j])
    return (hp.reshape(BATCH, SEQ, D_MODEL), hs.reshape(DEC_BATCH, 1, D_MODEL),
            *(jnp.stack(t) for t in st_p), *(jnp.stack(t) for t in st_s))
```

```python
import functools

import jax
import jax.numpy as jnp
import numpy as np
from jax import lax
from jax.experimental import pallas as pl
from jax.experimental.pallas import tpu as pltpu

F32 = jnp.float32
BF16 = jnp.bfloat16

D_MODEL = 1024
BATCH = 2
SEQ = 8192
DEPTH = 2
DEC_BATCH = 128
PAST_LEN = 16384
WINDOW = 128
A_HEADS = 8
A_KV_HEADS = 2
A_HEAD_DIM = 64
A_GROUP = A_HEADS // A_KV_HEADS
A_WIDTH = A_HEADS * A_HEAD_DIM
B_HEADS = 4
B_HEAD_DIM = 128
B_WIDTH = B_HEADS * B_HEAD_DIM
CONV_W = 4
GDN_CHUNK = 64
POOL_WINDOWS = (2, 4, 8, 16)
POOL_GROUPS = 4
C_WIDTH = 512
POOL_GROUP_DIM = C_WIDTH // POOL_GROUPS
POOL_HIST = max(POOL_WINDOWS) - 1
N_BRANCHES = 3
IN_SPLIT_SIZES = (A_WIDTH, A_KV_HEADS * A_HEAD_DIM, A_KV_HEADS * A_HEAD_DIM, 3 * B_WIDTH,
                  B_WIDTH, B_HEADS, B_HEADS, C_WIDTH, N_BRANCHES * D_MODEL)
N_GROUPS = 4
EXPERTS_PER_GROUP = 8
N_EXPERTS = N_GROUPS * EXPERTS_PER_GROUP
TOP_K = 2
EXPERT_FF = 512
DN_ALPHA = (2.0 * DEPTH) ** 0.25
LN_EPS = 1e-5
RMS_EPS = 1e-6
NEG_INF = -1e30

LANES = 128
SUBLANES = 8
ROW_TILES = D_MODEL // LANES
VMEM_LIMIT = 56 * 1024 * 1024

N_PROMPT = BATCH * SEQ
N_ROWS = N_PROMPT + DEC_BATCH

SEG_GATES = (0, N_BRANCHES * D_MODEL)
SEG_BQKV = (3072, 3 * B_WIDTH)
SEG_AQ = (4608, A_WIDTH)
SEG_Z = (5120, B_WIDTH)
SEG_CP = (5632, C_WIDTH)
SEG_KV = (6144, 2 * A_KV_HEADS * A_HEAD_DIM)
SEG_AB = (6400, LANES)
IN_PACKED = 6528

MOE_ROWS = 256
MOE_ASSIGN = N_ROWS * TOP_K
MOE_CAP = -(-MOE_ASSIGN // MOE_ROWS) * MOE_ROWS + N_EXPERTS * MOE_ROWS
MOE_BLOCKS = MOE_CAP // MOE_ROWS
ROUTER_COLS = N_GROUPS + N_EXPERTS


def _sigmoid(x):
    return 1.0 / (1.0 + jnp.exp(-x))


def _silu(x):
    return x * _sigmoid(x)


def _dot(a, b):
    return jnp.dot(a, b, preferred_element_type=F32)


def _dot_nt(a, b):
    return lax.dot_general(a, b, (((1,), (1,)), ((), ())), preferred_element_type=F32)


def _dot_tn(a, b):
    return lax.dot_general(a, b, (((0,), (0,)), ((), ())), preferred_element_type=F32)


def _split3(x):
    hi = x.astype(BF16)
    r = x - hi.astype(F32)
    mid = r.astype(BF16)
    lo = (r - mid.astype(F32)).astype(BF16)
    return hi, mid, lo


def _params(sem=None):
    return pltpu.CompilerParams(dimension_semantics=sem, vmem_limit_bytes=VMEM_LIMIT)


def _const_spec(shape):
    nd = len(shape)
    return pl.BlockSpec(shape, lambda *_: (0,) * nd)


def _pack_w_in(w):
    aq, ak, av, bqkv, bz, ba, bb, cp, gate = jnp.split(w, np.cumsum(IN_SPLIT_SIZES)[:-1].tolist(), axis=-1)
    aq = aq * (A_HEAD_DIM ** -0.5)
    ab = jnp.concatenate([ba, bb, jnp.zeros((D_MODEL, LANES - 2 * B_HEADS), w.dtype)], -1)
    return jnp.concatenate([gate, bqkv, aq, bz, cp, ak, av, ab], -1).astype(BF16)


def _in_proj_kernel(x_ref, w_ref, gates_ref, bqkv_ref, aq_ref, z_ref, cp_ref, kv_ref, ab_ref):
    x = x_ref[...].astype(BF16)

    def mm(lo, n):
        return _dot(x, w_ref[:, lo:lo + n])

    step = 512
    for c in range(SEG_GATES[1] // step):
        gates_ref[:, c * step:(c + 1) * step] = _sigmoid(mm(SEG_GATES[0] + c * step, step)).astype(BF16)
    for c in range(SEG_BQKV[1] // step):
        bqkv_ref[:, c * step:(c + 1) * step] = mm(SEG_BQKV[0] + c * step, step)
    aq_ref[...] = mm(*SEG_AQ).astype(BF16)
    z_ref[...] = mm(*SEG_Z)
    cp_ref[...] = mm(*SEG_CP)
    kv_ref[...] = mm(*SEG_KV)
    ab_ref[...] = mm(*SEG_AB)


def _in_proj(x, w_packed, tm):
    n = x.shape[0]
    widths = (SEG_GATES[1], SEG_BQKV[1], SEG_AQ[1], SEG_Z[1], SEG_CP[1], SEG_KV[1], SEG_AB[1])
    dtypes = (BF16, F32, BF16, F32, F32, F32, F32)
    return pl.pallas_call(
        _in_proj_kernel,
        grid=(n // tm,),
        in_specs=[pl.BlockSpec((tm, D_MODEL), lambda i: (i, 0)),
                  pl.BlockSpec((D_MODEL, IN_PACKED), lambda i: (0, 0), pipeline_mode=pl.Buffered(1))],
        out_specs=[pl.BlockSpec((tm, w), lambda i: (i, 0)) for w in widths],
        out_shape=[jax.ShapeDtypeStruct((n, w), dt) for w, dt in zip(widths, dtypes)],
        compiler_params=_params(("arbitrary",)),
        name="in_proj",
    )(x, w_packed)


def _alibi_slope(h):
    return 2.0 ** (-8.0 * (h + 1) / A_HEADS)


def _attn_prompt_kernel(sink_ref, q_ref, kvc_ref, kvp_ref, o_ref):
    n = pl.program_id(1)
    kvc = kvc_ref[...]
    kvp = kvp_ref[...]
    t = lax.broadcasted_iota(jnp.int32, (WINDOW, 2 * WINDOW), 0)
    s = lax.broadcasted_iota(jnp.int32, (WINDOW, 2 * WINDOW), 1)
    dist = WINDOW + t - s
    valid = (dist >= 0) & (dist <= WINDOW) & ((s >= WINDOW) | (n > 0))
    distf = dist.astype(F32)
    hd = A_HEAD_DIM
    vo = A_KV_HEADS * hd
    outs = []
    for kh in range(A_KV_HEADS):
        k = jnp.concatenate([kvp[:, kh * hd:(kh + 1) * hd], kvc[:, kh * hd:(kh + 1) * hd]], axis=0).astype(BF16)
        v = jnp.concatenate([kvp[:, vo + kh * hd:vo + (kh + 1) * hd],
                             kvc[:, vo + kh * hd:vo + (kh + 1) * hd]], axis=0).astype(BF16)
        for g in range(A_GROUP):
            h = kh * A_GROUP + g
            sc = _dot_nt(q_ref[:, h * hd:(h + 1) * hd], k)
            sc = jnp.where(valid, sc - _alibi_slope(h) * distf, NEG_INF)
            sink = sink_ref[h]
            m = jnp.maximum(jnp.max(sc, axis=-1, keepdims=True), sink)
            p = jnp.exp(sc - m)
            denom = jnp.sum(p, axis=-1, keepdims=True) + jnp.exp(sink - m)
            outs.append(_dot(p.astype(BF16), v) / denom)
    o_ref[...] = jnp.concatenate(outs, axis=-1).astype(BF16)


def _attn_prompt(sinks, aq, kv):
    nb = SEQ // WINDOW
    return pl.pallas_call(
        _attn_prompt_kernel,
        grid=(BATCH, nb),
        in_specs=[pl.BlockSpec(memory_space=pltpu.SMEM),
                  pl.BlockSpec((WINDOW, A_WIDTH), lambda b, n: (b * nb + n, 0)),
                  pl.BlockSpec((WINDOW, SEG_KV[1]), lambda b, n: (b * nb + n, 0)),
                  pl.BlockSpec((WINDOW, SEG_KV[1]), lambda b, n: (b * nb + jnp.maximum(n - 1, 0), 0))],
        out_specs=pl.BlockSpec((WINDOW, A_WIDTH), lambda b, n: (b * nb + n, 0)),
        out_shape=jax.ShapeDtypeStruct((N_PROMPT, A_WIDTH), BF16),
        compiler_params=_params(("arbitrary", "arbitrary")),
        name="attn_prompt",
    )(sinks, aq, kv, kv)


ATTN_S_BLOCK = 8


def _attn_sample_kernel(sink_ref, q_ref, kn_ref, vn_ref, ck_ref, cv_ref, o_ref, nk_ref, nv_ref):
    hd = A_HEAD_DIM
    row = lax.broadcasted_iota(jnp.int32, (A_HEADS, 1), 0)
    slope = jnp.zeros((A_HEADS, 1), F32)
    sink = jnp.zeros((A_HEADS, 1), F32)
    for h in range(A_HEADS):
        slope = jnp.where(row == h, _alibi_slope(h), slope)
        sink = jnp.where(row == h, sink_ref[h], sink)
    pos = lax.broadcasted_iota(jnp.int32, (A_HEADS, WINDOW), 1)
    bias = -slope * (WINDOW - pos).astype(F32)
    first = row < A_GROUP
    for b in range(ATTN_S_BLOCK):
        kc = ck_ref[b]
        vc = cv_ref[b]
        qf = q_ref[b]
        q = qf.astype(BF16)
        kn = kn_ref[b:b + 1, :]
        vn = vn_ref[b:b + 1, :]
        s0 = _dot_nt(q, kc[:, 0:hd].astype(BF16))
        s1 = _dot_nt(q, kc[:, hd:2 * hd].astype(BF16))
        sc = jnp.where(first, s0, s1) + bias
        sn0 = jnp.sum(qf * kn[:, 0:hd], axis=-1, keepdims=True)
        sn1 = jnp.sum(qf * kn[:, hd:2 * hd], axis=-1, keepdims=True)
        sn = jnp.where(first, sn0, sn1)
        m = jnp.maximum(jnp.maximum(jnp.max(sc, axis=-1, keepdims=True), sn), sink)
        p = jnp.exp(sc - m)
        pn = jnp.exp(sn - m)
        denom = jnp.sum(p, axis=-1, keepdims=True) + pn + jnp.exp(sink - m)
        pb = p.astype(BF16)
        o0 = _dot(pb, vc[:, 0:hd].astype(BF16)) + pn * vn[:, 0:hd]
        o1 = _dot(pb, vc[:, hd:2 * hd].astype(BF16)) + pn * vn[:, hd:2 * hd]
        o_ref[b] = jnp.where(first, o0, o1) / denom
        nk_ref[b, 0:WINDOW - 1, :] = kc[1:WINDOW, :]
        nk_ref[b, WINDOW - 1:WINDOW, :] = kn
        nv_ref[b, 0:WINDOW - 1, :] = vc[1:WINDOW, :]
        nv_ref[b, WINDOW - 1:WINDOW, :] = vn


def _attn_sample(layer, sinks, q3, kn, vn, ck, cv):
    bs = ATTN_S_BLOCK
    kvw = A_KV_HEADS * A_HEAD_DIM
    return pl.pallas_call(
        _attn_sample_kernel,
        grid=(DEC_BATCH // bs,),
        in_specs=[pl.BlockSpec(memory_space=pltpu.SMEM),
                  pl.BlockSpec((bs, A_HEADS, A_HEAD_DIM), lambda i: (i, 0, 0)),
                  pl.BlockSpec((bs, kvw), lambda i: (i, 0)),
                  pl.BlockSpec((bs, kvw), lambda i: (i, 0)),
                  pl.BlockSpec((None, bs, WINDOW, kvw), lambda i: (layer, i, 0, 0)),
                  pl.BlockSpec((None, bs, WINDOW, kvw), lambda i: (layer, i, 0, 0))],
        out_specs=[pl.BlockSpec((bs, A_HEADS, A_HEAD_DIM), lambda i: (i, 0, 0)),
                   pl.BlockSpec((bs, WINDOW, kvw), lambda i: (i, 0, 0)),
                   pl.BlockSpec((bs, WINDOW, kvw), lambda i: (i, 0, 0))],
        out_shape=[jax.ShapeDtypeStruct((DEC_BATCH, A_HEADS, A_HEAD_DIM), F32),
                   jax.ShapeDtypeStruct((DEC_BATCH, WINDOW, kvw), F32),
                   jax.ShapeDtypeStruct((DEC_BATCH, WINDOW, kvw), F32)],
        compiler_params=_params(("arbitrary",)),
        name="attn_sample",
    )(sinks, q3, kn, vn, ck, cv)


GDN_TILE = 256
GDN_STACK = B_HEADS * GDN_CHUNK


def _gdn_gates(ab, alog, dtb):
    xs = ab + dtb
    softplus = jnp.maximum(xs, 0.0) + jnp.log(1.0 + jnp.exp(-jnp.abs(xs)))
    g = -jnp.exp(alog) * softplus
    beta = _sigmoid(ab)
    return g, beta


def _l2norm(x):
    return x * lax.rsqrt(jnp.sum(x * x, axis=-1, keepdims=True) + RMS_EPS)


def _gdn_out_norm(o, z, norm_g):
    return o * lax.rsqrt(jnp.mean(o * o, axis=-1, keepdims=True) + RMS_EPS) * norm_g * _silu(z)


def _gdn_prompt_kernel(x_ref, z_ref, ab_ref, cw_ref, alog_ref, dtb_ref, ng_ref,
                       o_ref, sfin_ref, xbuf_ref, s_ref):
    n = pl.program_id(1)
    t_rows = GDN_TILE
    c_rows = GDN_CHUNK
    hd = B_HEAD_DIM
    pad = SUBLANES

    @pl.when(n == 0)
    def _():
        xbuf_ref[0:pad, :] = jnp.zeros((pad, 3 * B_WIDTH), F32)
        s_ref[...] = jnp.zeros_like(s_ref)

    x = x_ref[...]
    xbuf_ref[pad:pad + t_rows, :] = x
    cw = cw_ref[...]
    conv = x * cw[CONV_W - 1:CONV_W, :]
    for j in range(CONV_W - 1):
        back = CONV_W - 1 - j
        conv = conv + xbuf_ref[pad - back:pad - back + t_rows, :] * cw[j:j + 1, :]
    xbuf_ref[0:pad, :] = x[t_rows - pad:t_rows, :]
    conv = _silu(conv)

    g_all, beta_all = _gdn_gates(ab_ref[...], alog_ref[...], dtb_ref[...])
    ri = lax.broadcasted_iota(jnp.int32, (t_rows, t_rows), 0)
    ci = lax.broadcasted_iota(jnp.int32, (t_rows, t_rows), 1)
    tri = jnp.where((ri // c_rows == ci // c_rows) & (ri >= ci), 1.0, 0.0).astype(BF16)
    g_hi, g_mid, g_lo = _split3(g_all)
    gcum_all = _dot(tri, g_hi) + _dot(tri, g_mid) + _dot(tri, g_lo)

    qn, kn, vv = [], [], []
    for h in range(B_HEADS):
        qn.append(_l2norm(conv[:, h * hd:(h + 1) * hd]) * (hd ** -0.5))
        kn.append(_l2norm(conv[:, B_WIDTH + h * hd:B_WIDTH + (h + 1) * hd]))
        vv.append(conv[:, 2 * B_WIDTH + h * hd:2 * B_WIDTH + (h + 1) * hd])

    st = GDN_STACK
    r = lax.broadcasted_iota(jnp.int32, (st, st), 0)
    c = lax.broadcasted_iota(jnp.int32, (st, st), 1)
    same = (r // c_rows) == (c // c_rows)
    incl = same & (r >= c)
    strict = same & (r > c)
    eye = (r == c)

    def stack(parts, lo):
        return jnp.concatenate([p[lo:lo + c_rows] for p in parts], axis=0)

    o_chunks = []
    for ck in range(t_rows // c_rows):
        lo = ck * c_rows
        qs, ks, vs = stack(qn, lo), stack(kn, lo), stack(vv, lo)
        gcol = jnp.concatenate([gcum_all[lo:lo + c_rows, h:h + 1] for h in range(B_HEADS)], axis=0)
        bcol = jnp.concatenate([beta_all[lo:lo + c_rows, B_HEADS + h:B_HEADS + h + 1] for h in range(B_HEADS)], axis=0)
        glast = [gcum_all[lo + c_rows - 1:lo + c_rows, h:h + 1] for h in range(B_HEADS)]
        glast_col = jnp.concatenate([jnp.broadcast_to(gl, (c_rows, 1)) for gl in glast], axis=0)
        e_mat = jnp.broadcast_to(gcol, (st, st))
        diff = e_mat - e_mat.T
        decay = jnp.where(incl, jnp.exp(jnp.where(incl, diff, 0.0)), 0.0)
        exp_g = jnp.exp(gcol)
        kb = ks * bcol
        vb = vs * bcol
        ks16 = ks.astype(BF16)
        kk = _dot_nt(kb.astype(BF16), ks16)
        qk = _dot_nt(qs.astype(BF16), ks16)
        m_mat = jnp.where(strict, kk * decay, 0.0)
        p_inv = None
        blk = 1
        while blk < c_rows:
            lvl = ((r // (2 * blk)) == (c // (2 * blk))) & ((r // blk) != (c // blk)) & (r > c)
            a_off = jnp.where(lvl, m_mat, 0.0)
            if p_inv is None:
                p_inv = jnp.where(eye, 1.0, 0.0) - a_off
            else:
                p16 = p_inv.astype(BF16)
                p_inv = p_inv - _dot(_dot(p16, a_off.astype(BF16)).astype(BF16), p16)
            blk *= 2
        uw = _dot(p_inv.astype(BF16), jnp.concatenate([vb, kb * exp_g], axis=1).astype(BF16))
        u = uw[:, 0:hd]
        w = uw[:, hd:2 * hd]
        aqk = qk * decay
        qg = qs * exp_g
        kd = ks * jnp.exp(glast_col - gcol)
        vnew, qs_s = [], []
        for h in range(B_HEADS):
            hs = slice(h * c_rows, (h + 1) * c_rows)
            wq = jnp.concatenate([w[hs], qg[hs]], axis=0).astype(BF16)
            rs = _dot(wq, s_ref[h].astype(BF16))
            vnew.append(u[hs] - rs[0:c_rows])
            qs_s.append(rs[c_rows:2 * c_rows])
        vnew_st = jnp.concatenate(vnew, axis=0)
        o_st = jnp.concatenate(qs_s, axis=0) + _dot(aqk.astype(BF16), vnew_st.astype(BF16))
        for h in range(B_HEADS):
            hs = slice(h * c_rows, (h + 1) * c_rows)
            s_ref[h] = s_ref[h] * jnp.exp(glast[h]) + _dot_tn(kd[hs].astype(BF16), vnew[h].astype(BF16))
        o_chunks.append([o_st[h * c_rows:(h + 1) * c_rows] for h in range(B_HEADS)])

    ng = ng_ref[...]
    z = z_ref[...]
    outs = []
    for h in range(B_HEADS):
        o_h = jnp.concatenate([oc[h] for oc in o_chunks], axis=0)
        outs.append(_gdn_out_norm(o_h, z[:, h * hd:(h + 1) * hd], ng))
    o_ref[...] = jnp.concatenate(outs, axis=-1).astype(BF16)
    sfin_ref[0] = s_ref[...]


def _gdn_prompt(bqkv, z, ab, conv_w, alog_row, dtb_row, norm_g):
    nt = SEQ // GDN_TILE
    t = GDN_TILE
    return pl.pallas_call(
        _gdn_prompt_kernel,
        grid=(BATCH, nt),
        in_specs=[pl.BlockSpec((t, 3 * B_WIDTH), lambda b, n: (b * nt + n, 0)),
                  pl.BlockSpec((t, B_WIDTH), lambda b, n: (b * nt + n, 0)),
                  pl.BlockSpec((t, LANES), lambda b, n: (b * nt + n, 0)),
                  _const_spec((CONV_W, 3 * B_WIDTH)),
                  _const_spec((1, LANES)),
                  _const_spec((1, LANES)),
                  _const_spec((1, B_HEAD_DIM))],
        out_specs=[pl.BlockSpec((t, B_WIDTH), lambda b, n: (b * nt + n, 0)),
                   pl.BlockSpec((1, B_HEADS, B_HEAD_DIM, B_HEAD_DIM), lambda b, n: (b, 0, 0, 0))],
        out_shape=[jax.ShapeDtypeStruct((N_PROMPT, B_WIDTH), BF16),
                   jax.ShapeDtypeStruct((BATCH, B_HEADS, B_HEAD_DIM, B_HEAD_DIM), F32)],
        scratch_shapes=[pltpu.VMEM((SUBLANES + t, 3 * B_WIDTH), F32),
                        pltpu.VMEM((B_HEADS, B_HEAD_DIM, B_HEAD_DIM), F32)],
        compiler_params=_params(("arbitrary", "arbitrary")),
        name="gdn_prompt",
    )(bqkv, z, ab, conv_w, alog_row, dtb_row, norm_g)


GDN_S_BLOCK = 8


def _gdn_sample_kernel(st_ref, x_ref, z_ref, ab_ref, cw_ref, alog_ref, dtb_ref, ng_ref, s_ref,
                       o_ref, nst_ref, ns_ref):
    hd = B_HEAD_DIM
    x = x_ref[...]
    cw = cw_ref[...]
    conv = x * cw[CONV_W - 1:CONV_W, :]
    for j in range(CONV_W - 1):
        conv = conv + st_ref[j] * cw[j:j + 1, :]
    for j in range(CONV_W - 2):
        nst_ref[j] = st_ref[j + 1]
    nst_ref[CONV_W - 2] = x
    conv = _silu(conv)
    g_all, beta_all = _gdn_gates(ab_ref[...], alog_ref[...], dtb_ref[...])
    eg_all = jnp.exp(g_all)
    r = lax.broadcasted_iota(jnp.int32, (hd, hd), 0)
    c = lax.broadcasted_iota(jnp.int32, (hd, hd), 1)
    eye = r == c
    zpad = jnp.zeros((SUBLANES - 2, hd), F32)
    ng = ng_ref[...]
    z = z_ref[...]
    outs = []
    for h in range(B_HEADS):
        q = _l2norm(conv[:, h * hd:(h + 1) * hd]) * (hd ** -0.5)
        k = _l2norm(conv[:, B_WIDTH + h * hd:B_WIDTH + (h + 1) * hd])
        v = conv[:, 2 * B_WIDTH + h * hd:2 * B_WIDTH + (h + 1) * hd]
        beta = beta_all[:, B_HEADS + h:B_HEADS + h + 1]
        eg = eg_all[:, h:h + 1]
        qk = jnp.sum(q * k, axis=-1, keepdims=True)
        rows = []
        for b in range(GDN_S_BLOCK):
            kb = k[b:b + 1]
            s_bh = s_ref[b, h]
            kq = jnp.concatenate([kb, q[b:b + 1], zpad], axis=0).astype(BF16)
            rs = _dot(kq, s_bh.astype(BF16))
            bb = beta[b:b + 1]
            egb = eg[b:b + 1]
            v_new = bb * v[b:b + 1] - (bb * egb) * rs[0:1]
            rows.append(egb * rs[1:2] + qk[b:b + 1] * v_new)
            kdiag = jnp.where(eye, jnp.broadcast_to(kb, (hd, hd)), 0.0).astype(BF16)
            outer = _dot(kdiag, jnp.broadcast_to(v_new, (hd, hd)).astype(BF16))
            ns_ref[b, h] = s_bh * egb + outer
        o_h = jnp.concatenate(rows, axis=0)
        outs.append(_gdn_out_norm(o_h, z[:, h * hd:(h + 1) * hd], ng))
    o_ref[...] = jnp.concatenate(outs, axis=-1).astype(BF16)


def _gdn_sample(layer, conv_st, bqkv, z, ab, conv_w, alog_row, dtb_row, norm_g, s0):
    bs = GDN_S_BLOCK
    cw3 = 3 * B_WIDTH
    return pl.pallas_call(
        _gdn_sample_kernel,
        grid=(DEC_BATCH // bs,),
        in_specs=[pl.BlockSpec((CONV_W - 1, bs, cw3), lambda i: (0, i, 0)),
                  pl.BlockSpec((bs, cw3), lambda i: (i, 0)),
                  pl.BlockSpec((bs, B_WIDTH), lambda i: (i, 0)),
                  pl.BlockSpec((bs, LANES), lambda i: (i, 0)),
                  _const_spec((CONV_W, cw3)),
                  _const_spec((1, LANES)),
                  _const_spec((1, LANES)),
                  _const_spec((1, B_HEAD_DIM)),
                  pl.BlockSpec((None, bs, B_HEADS, B_HEAD_DIM, B_HEAD_DIM), lambda i: (layer, i, 0, 0, 0))],
        out_specs=[pl.BlockSpec((bs, B_WIDTH), lambda i: (i, 0)),
                   pl.BlockSpec((CONV_W - 1, bs, cw3), lambda i: (0, i, 0)),
                   pl.BlockSpec((bs, B_HEADS, B_HEAD_DIM, B_HEAD_DIM), lambda i: (i, 0, 0, 0))],
        out_shape=[jax.ShapeDtypeStruct((DEC_BATCH, B_WIDTH), BF16),
                   jax.ShapeDtypeStruct((CONV_W - 1, DEC_BATCH, cw3), F32),
                   jax.ShapeDtypeStruct((DEC_BATCH, B_HEADS, B_HEAD_DIM, B_HEAD_DIM), F32)],
        compiler_params=_params(("arbitrary",)),
        name="gdn_sample",
    )(conv_st, bqkv, z, ab, conv_w, alog_row, dtb_row, norm_g, s0)


POOL_TILE = 512
POOL_PAD = 16


def _pool_project(d, pw_ref, scale):
    outs = []
    for gi in range(POOL_GROUPS):
        lo = gi * POOL_GROUP_DIM
        outs.append(_dot(d[:, lo:lo + POOL_GROUP_DIM].astype(BF16), pw_ref[gi].astype(BF16)))
    return jnp.concatenate(outs, axis=-1) * scale


def _pool_prompt_kernel(x_ref, pw_ref, ps_ref, o_ref, buf_ref):
    n = pl.program_id(1)
    t = POOL_TILE
    pad = POOL_PAD
    gd = POOL_GROUP_DIM

    @pl.when(n == 0)
    def _():
        buf_ref[0:pad, :] = jnp.zeros((pad, C_WIDTH), F32)

    x = x_ref[...]
    buf_ref[pad:pad + t, :] = x
    s1 = buf_ref[1:pad + t, :]
    s2 = s1 + buf_ref[0:pad + t - 1, :]
    s4 = s2[2:, gd:] + s2[:-2, gd:]
    s8 = s4[4:, gd:] + s4[:-4, gd:]
    s16 = s8[8:, gd:] + s8[:-8, gd:]
    sums = (s2[15:, 0:gd], s4[13:, 0:gd], s8[9:, 0:gd], s16[1:, :])
    pos = n * t + lax.broadcasted_iota(jnp.int32, (t, 1), 0) + 1
    means = []
    for gi, w in enumerate(POOL_WINDOWS):
        cnt = jnp.minimum(pos, w).astype(F32)
        means.append(sums[gi] / cnt)
    d = jnp.concatenate(means, axis=-1) - x
    o_ref[...] = _pool_project(d, pw_ref, ps_ref[...]).astype(BF16)
    buf_ref[0:pad, :] = x[t - pad:t, :]


def _pool_prompt(cp, pool_w, pool_scale):
    nt = SEQ // POOL_TILE
    t = POOL_TILE
    return pl.pallas_call(
        _pool_prompt_kernel,
        grid=(BATCH, nt),
        in_specs=[pl.BlockSpec((t, C_WIDTH), lambda b, n: (b * nt + n, 0)),
                  _const_spec((POOL_GROUPS, POOL_GROUP_DIM, POOL_GROUP_DIM)),
                  _const_spec((1, C_WIDTH))],
        out_specs=pl.BlockSpec((t, C_WIDTH), lambda b, n: (b * nt + n, 0)),
        out_shape=jax.ShapeDtypeStruct((N_PROMPT, C_WIDTH), BF16),
        scratch_shapes=[pltpu.VMEM((POOL_PAD + t, C_WIDTH), F32)],
        compiler_params=_params(("arbitrary", "arbitrary")),
        name="pool_prompt",
    )(cp, pool_w, pool_scale)


def _pool_sample_kernel(st_ref, x_ref, pw_ref, ps_ref, o_ref, nst_ref):
    x = x_ref[...]
    gd = POOL_GROUP_DIM
    means = []
    for gi, w in enumerate(POOL_WINDOWS):
        lo = gi * gd
        acc = x[:, lo:lo + gd]
        for j in range(1, w):
            acc = acc + st_ref[POOL_HIST - j][:, lo:lo + gd]
        means.append(acc / float(min(PAST_LEN + 1, w)))
    d = jnp.concatenate(means, axis=-1) - x
    o_ref[...] = _pool_project(d, pw_ref, ps_ref[...]).astype(BF16)
    for j in range(POOL_HIST - 1):
        nst_ref[j] = st_ref[j + 1]
    nst_ref[POOL_HIST - 1] = x


def _pool_sample(pool_st, cp, pool_w, pool_scale):
    return pl.pallas_call(
        _pool_sample_kernel,
        out_shape=[jax.ShapeDtypeStruct((DEC_BATCH, C_WIDTH), BF16),
                   jax.ShapeDtypeStruct((POOL_HIST, DEC_BATCH, C_WIDTH), F32)],
        compiler_params=_params(),
        name="pool_sample",
    )(pool_st, cp, pool_w, pool_scale)


def _layer_norm(y, g, b):
    mu = jnp.mean(y, axis=-1, keepdims=True)
    yc = y - mu
    var = jnp.mean(yc * yc, axis=-1, keepdims=True)
    return yc * lax.rsqrt(var + LN_EPS) * g + b


def _store_token_tiled(ref, val, rows):
    for s in range(ROW_TILES):
        ref[pl.ds(s, rows, stride=ROW_TILES), :] = val[:, s * LANES:(s + 1) * LANES]


def _load_token_tiled(ref, base, rows):
    return jnp.concatenate([ref[pl.ds(base + s, rows, stride=ROW_TILES), :] for s in range(ROW_TILES)], axis=-1)


def _merge_kernel(hp_ref, oap_ref, obp_ref, ocp_ref, gp_ref, hs_ref, oas_ref, obs_ref, ocs_ref, gs_ref,
                  wa_ref, wb_ref, wc_ref, wo_ref, lng_ref, lnb_ref, wr_ref, br_ref,
                  h1_ref, lg_ref, *, n_prompt_tiles):
    i = pl.program_id(0)

    def run(h_ref, oa_ref, ob_ref, oc_ref, g_ref):
        rows = h_ref.shape[0]
        merged = (g_ref[:, 0:D_MODEL].astype(F32) * _dot(oa_ref[...], wa_ref[...])
                  + g_ref[:, D_MODEL:2 * D_MODEL].astype(F32) * _dot(ob_ref[...], wb_ref[...])
                  + g_ref[:, 2 * D_MODEL:3 * D_MODEL].astype(F32) * _dot(oc_ref[...], wc_ref[...]))
        mix = _dot(merged.astype(BF16), wo_ref[...])
        h1 = _layer_norm(DN_ALPHA * h_ref[...] + mix, lng_ref[...], lnb_ref[...])
        h1_ref[0:rows, :] = h1
        hh, hm, _ = _split3(h1)
        wh, wm, _ = _split3(wr_ref[...])
        lg_ref[0:rows, :] = _dot(hh, wh) + _dot(hh, wm) + _dot(hm, wh) + br_ref[...]

    @pl.when(i < n_prompt_tiles)
    def _():
        run(hp_ref, oap_ref, obp_ref, ocp_ref, gp_ref)

    @pl.when(i == n_prompt_tiles)
    def _():
        run(hs_ref, oas_ref, obs_ref, ocs_ref, gs_ref)


def _merge(prompt, sample, wa, wb, wc, wo, lng, lnb, wr, br, tm):
    n_p = N_PROMPT // tm
    last = n_p - 1
    widths = (D_MODEL, A_WIDTH, B_WIDTH, C_WIDTH, N_BRANCHES * D_MODEL)
    in_specs = ([pl.BlockSpec((tm, w), lambda i: (jnp.minimum(i, last), 0)) for w in widths]
                + [_const_spec((DEC_BATCH, w)) for w in widths]
                + [_const_spec((A_WIDTH, D_MODEL)), _const_spec((B_WIDTH, D_MODEL)), _const_spec((C_WIDTH, D_MODEL)),
                   _const_spec((D_MODEL, D_MODEL)), _const_spec((1, D_MODEL)), _const_spec((1, D_MODEL)),
                   _const_spec((D_MODEL, LANES)), _const_spec((1, LANES))])
    return pl.pallas_call(
        functools.partial(_merge_kernel, n_prompt_tiles=n_p),
        grid=(n_p + 1,),
        in_specs=in_specs,
        out_specs=[pl.BlockSpec((tm, D_MODEL), lambda i: (i, 0)),
                   pl.BlockSpec((tm, LANES), lambda i: (i, 0))],
        out_shape=[jax.ShapeDtypeStruct((N_ROWS, D_MODEL), F32),
                   jax.ShapeDtypeStruct((N_ROWS, LANES), F32)],
        compiler_params=_params(("arbitrary",)),
        name="merge",
    )(*prompt, *sample, wa, wb, wc, wo, lng, lnb, wr, br)


ROUTE_TILE = 128
REC_EXPERT = 0
REC_RANK = 2
REC_WEIGHT = 4


def _route_kernel(lg_ref, rec_ref, sizes_ref, carry_ref):
    i = pl.program_id(0)
    t_rows = ROUTE_TILE

    @pl.when(i == 0)
    def _():
        carry_ref[...] = jnp.zeros_like(carry_ref)

    lg = lg_ref[...]
    lane = lax.broadcasted_iota(jnp.int32, (t_rows, LANES), 1).astype(F32)
    far = float(LANES)

    def masked_argmax(mask):
        v = jnp.max(jnp.where(mask, lg, NEG_INF), axis=-1, keepdims=True)
        idx = jnp.min(jnp.where(mask & (lg == v), lane, far), axis=-1, keepdims=True)
        return v, idx

    gmask = lane < N_GROUPS
    gmax, gsel = masked_argmax(gmask)
    gw = 1.0 / jnp.sum(jnp.where(gmask, jnp.exp(lg - gmax), 0.0), axis=-1, keepdims=True)
    lo = N_GROUPS + EXPERTS_PER_GROUP * gsel
    emask = (lane >= lo) & (lane < lo + EXPERTS_PER_GROUP)
    v1, i1 = masked_argmax(emask)
    v2, i2 = masked_argmax(emask & (lane != i1))
    t = jnp.exp(v2 - v1)
    w1 = gw / (1.0 + t)
    w2 = gw * t / (1.0 + t)
    e1 = i1 - N_GROUPS
    e2 = i2 - N_GROUPS

    o1 = jnp.where(lane == e1, 1.0, 0.0)
    o2 = jnp.where(lane == e2, 1.0, 0.0)
    r = lax.broadcasted_iota(jnp.int32, (t_rows, t_rows), 0)
    c = lax.broadcasted_iota(jnp.int32, (t_rows, t_rows), 1)
    before = jnp.where(c < r, 1.0, 0.0).astype(BF16)
    carry = carry_ref[...]
    tot1 = jnp.sum(o1, axis=0, keepdims=True)
    tot2 = jnp.sum(o2, axis=0, keepdims=True)
    rank1 = jnp.sum(o1 * (_dot(before, o1.astype(BF16)) + carry), axis=-1, keepdims=True)
    rank2 = jnp.sum(o2 * (_dot(before, o2.astype(BF16)) + (carry + tot1)), axis=-1, keepdims=True)
    carry = carry + tot1 + tot2
    carry_ref[...] = carry
    sizes_ref[...] = carry

    rec = jnp.zeros((t_rows, LANES), F32)
    for k, val in enumerate((e1, e2, rank1, rank2, w1, w2)):
        rec = jnp.where(lane == k, val, rec)
    rec_ref[...] = rec


def _route(logits):
    rec, sizes_row = pl.pallas_call(
        _route_kernel,
        grid=(N_ROWS // ROUTE_TILE,),
        in_specs=[pl.BlockSpec((ROUTE_TILE, LANES), lambda i: (i, 0))],
        out_specs=[pl.BlockSpec((ROUTE_TILE, LANES), lambda i: (i, 0)), _const_spec((1, LANES))],
        out_shape=[jax.ShapeDtypeStruct((N_ROWS, LANES), F32), jax.ShapeDtypeStruct((1, LANES), F32)],
        scratch_shapes=[pltpu.VMEM((1, LANES), F32)],
        compiler_params=_params(("arbitrary",)),
        name="route",
    )(logits)
    eid = rec[:, REC_EXPERT:REC_EXPERT + TOP_K].astype(jnp.int32)
    rank = rec[:, REC_RANK:REC_RANK + TOP_K].astype(jnp.int32)
    sizes = sizes_row[0, :N_EXPERTS].astype(jnp.int32)
    padded = (sizes + MOE_ROWS - 1) // MOE_ROWS * MOE_ROWS
    end_padded = jnp.cumsum(padded).astype(jnp.int32)
    start_padded = end_padded - padded
    experts = jnp.arange(N_EXPERTS, dtype=jnp.int32)
    dest = jnp.sum(jnp.where(eid[:, :, None] == experts, start_padded, 0), axis=-1) + rank
    blk_start = jnp.arange(MOE_BLOCKS, dtype=jnp.int32) * MOE_ROWS
    n_used = end_padded[-1] // MOE_ROWS
    blk_exp = jnp.sum((blk_start[:, None] >= end_padded[None, :]).astype(jnp.int32), axis=-1)
    last_exp = jnp.sum((blk_start[jnp.maximum(n_used - 1, 0)] >= end_padded).astype(jnp.int32))
    blk_exp = jnp.where(blk_start < end_padded[-1], blk_exp, last_exp).astype(jnp.int32)
    return rec, dest, end_padded, padded.astype(jnp.int32), blk_exp, n_used.reshape(1).astype(jnp.int32)


def _tile_rows(ref, j):
    return ref.at[pl.ds(pl.multiple_of(j * ROW_TILES, ROW_TILES), ROW_TILES), :]


def _gather_rows(idx_ref, src_ref, dst_ref, sem, count):
    def body(jj, carry):
        for p in range(2):
            j = 2 * jj + p
            pltpu.make_async_copy(_tile_rows(src_ref, idx_ref[0, 0, j]), _tile_rows(dst_ref, j), sem).start(priority=p)
        return carry
    lax.fori_loop(0, count // 2, body, 0)


def _gather_wait(src_ref, dst_ref, sem, count):
    pltpu.make_async_copy(src_ref.at[pl.ds(0, count * ROW_TILES), :], dst_ref, sem).wait()


DISPATCH_TILE = 384


def _dispatch_kernel(endp_ref, padded_ref, nused_ref, idx_ref, h_ref, xb_hbm, stage_ref, zero_ref, sem_ref, zsem_ref):
    i = pl.program_id(0)
    n = pl.num_programs(0)
    tm = h_ref.shape[0]
    slot = i % 2
    blk = MOE_ROWS * ROW_TILES

    def zero_copy(b0):
        return pltpu.make_async_copy(zero_ref, xb_hbm.at[pl.ds(pl.multiple_of(b0 * ROW_TILES, ROW_TILES), blk), :], zsem_ref.at[0])

    @pl.when(i == 0)
    def _():
        zero_ref[...] = jnp.zeros_like(zero_ref)
        for e in range(N_EXPERTS):
            @pl.when(padded_ref[e] > 0)
            def _():
                zero_copy(endp_ref[e] - MOE_ROWS).start()

        def tail_start(b, carry):
            zero_copy(b * MOE_ROWS).start()
            return carry
        lax.fori_loop(nused_ref[0], MOE_BLOCKS, tail_start, 0)
        for e in range(N_EXPERTS):
            @pl.when(padded_ref[e] > 0)
            def _():
                zero_copy(endp_ref[e] - MOE_ROWS).wait()

        def tail_wait(b, carry):
            zero_copy(b * MOE_ROWS).wait()
            return carry
        lax.fori_loop(nused_ref[0], MOE_BLOCKS, tail_wait, 0)

    def wait_slot(s):
        for _ in range(TOP_K):
            pltpu.make_async_copy(stage_ref.at[s], xb_hbm.at[pl.ds(0, tm * ROW_TILES), :], sem_ref.at[s]).wait()

    @pl.when(i >= 2)
    def _():
        wait_slot(slot)

    stage = stage_ref.at[slot]
    _store_token_tiled(stage, h_ref[...], tm)

    def body(t, carry):
        for k in range(TOP_K):
            pltpu.make_async_copy(_tile_rows(stage, t), _tile_rows(xb_hbm, idx_ref[0, 0, k * tm + t]),
                                  sem_ref.at[slot]).start(priority=k)
        return carry
    lax.fori_loop(0, tm, body, 0)

    @pl.when(i == n - 1)
    def _():
        wait_slot(slot)

        @pl.when(n > 1)
        def _():
            wait_slot(1 - slot)


def _tile_major_idx(dest, n_tiles, tm):
    idx = dest.reshape(n_tiles, tm, TOP_K)
    return jnp.transpose(idx, (0, 2, 1)).reshape(n_tiles, 1, TOP_K * tm)


def _dispatch(end_padded, padded, n_used, dest, h1):
    tm = DISPATCH_TILE
    n_tiles = N_ROWS // tm
    idx = _tile_major_idx(dest, n_tiles, tm)
    grid_spec = pltpu.PrefetchScalarGridSpec(
        num_scalar_prefetch=3,
        grid=(n_tiles,),
        in_specs=[pl.BlockSpec((1, 1, TOP_K * tm), lambda i, *_: (i, 0, 0), memory_space=pltpu.SMEM),
                  pl.BlockSpec((tm, D_MODEL), lambda i, *_: (i, 0))],
        out_specs=pl.BlockSpec(memory_space=pl.ANY),
        scratch_shapes=[pltpu.VMEM((2, tm * ROW_TILES, LANES), F32),
                        pltpu.VMEM((MOE_ROWS * ROW_TILES, LANES), F32),
                        pltpu.SemaphoreType.DMA((2,)),
                        pltpu.SemaphoreType.DMA((1,))])
    return pl.pallas_call(
        _dispatch_kernel,
        grid_spec=grid_spec,
        out_shape=jax.ShapeDtypeStruct((MOE_CAP * ROW_TILES, LANES), F32),
        compiler_params=_params(("arbitrary",)),
        name="dispatch",
    )(end_padded, padded, n_used, idx, h1)


def _expert_kernel(bexp_ref, nused_ref, x_ref, wg_ref, wu_ref, wd_ref, y_ref, wg16_ref, wu16_ref, wd16_ref):
    i = pl.program_id(0)
    n_used = nused_ref[0]
    rows = MOE_ROWS

    @pl.when((i == 0) | (bexp_ref[i] != bexp_ref[jnp.maximum(i - 1, 0)]))
    def _():
        wg16_ref[...] = wg_ref[...].astype(BF16)
        wu16_ref[...] = wu_ref[...].astype(BF16)
        wd16_ref[...] = wd_ref[...].astype(BF16)

    @pl.when(i < n_used)
    def _():
        x = _load_token_tiled(x_ref, 0, rows).astype(BF16)
        hid = _silu(_dot(x, wg16_ref[...])) * _dot(x, wu16_ref[...])
        _store_token_tiled(y_ref, _dot(hid.astype(BF16), wd16_ref[...]), rows)

    @pl.when(i >= n_used)
    def _():
        y_ref[...] = jnp.zeros_like(y_ref)


def _experts(layer, blk_exp, n_used, xb, w_gate, w_up, w_down):
    rows = MOE_ROWS
    blk = (rows * ROW_TILES, LANES)
    grid_spec = pltpu.PrefetchScalarGridSpec(
        num_scalar_prefetch=2,
        grid=(MOE_BLOCKS,),
        in_specs=[pl.BlockSpec(blk, lambda i, be, nu: (jnp.minimum(i, nu[0] - 1), 0)),
                  pl.BlockSpec((None, None, D_MODEL, EXPERT_FF), lambda i, be, nu: (layer, be[i], 0, 0)),
                  pl.BlockSpec((None, None, D_MODEL, EXPERT_FF), lambda i, be, nu: (layer, be[i], 0, 0)),
                  pl.BlockSpec((None, None, EXPERT_FF, D_MODEL), lambda i, be, nu: (layer, be[i], 0, 0))],
        out_specs=pl.BlockSpec(blk, lambda i, be, nu: (i, 0)),
        scratch_shapes=[pltpu.VMEM((D_MODEL, EXPERT_FF), BF16),
                        pltpu.VMEM((D_MODEL, EXPERT_FF), BF16),
                        pltpu.VMEM((EXPERT_FF, D_MODEL), BF16)])
    return pl.pallas_call(
        _expert_kernel,
        grid_spec=grid_spec,
        out_shape=jax.ShapeDtypeStruct((MOE_CAP * ROW_TILES, LANES), F32),
        compiler_params=_params(("arbitrary",)),
        name="experts",
    )(blk_exp, n_used, xb, w_gate, w_up, w_down)


def _combine_kernel(idx_ref, idxn_ref, h1_ref, rec_ref, y_hbm, lng_ref, lnb_ref, o_ref, ybuf_ref, sem_ref, *, n_tiles):
    i = pl.program_id(0)
    tm = h1_ref.shape[0]
    slot = i % 2
    count = TOP_K * tm

    @pl.when(i == 0)
    def _():
        _gather_rows(idx_ref, y_hbm, ybuf_ref.at[0], sem_ref.at[0], count)

    @pl.when(i + 1 < n_tiles)
    def _():
        _gather_rows(idxn_ref, y_hbm, ybuf_ref.at[1 - slot], sem_ref.at[1 - slot], count)

    _gather_wait(y_hbm, ybuf_ref.at[slot], sem_ref.at[slot], count)
    y = DN_ALPHA * h1_ref[...]
    for k in range(TOP_K):
        wk = rec_ref[:, REC_WEIGHT + k:REC_WEIGHT + k + 1]
        y = y + wk * _load_token_tiled(ybuf_ref.at[slot], k * tm * ROW_TILES, tm)
    o_ref[...] = _layer_norm(y, lng_ref[...], lnb_ref[...])


def _combine(dest, rec, h1, yb, lng, lnb, tm, row0, n):
    n_tiles = n // tm
    blk0 = row0 // tm
    idx = _tile_major_idx(dest[row0:row0 + n], n_tiles, tm)
    last = n_tiles - 1
    return pl.pallas_call(
        functools.partial(_combine_kernel, n_tiles=n_tiles),
        grid=(n_tiles,),
        in_specs=[pl.BlockSpec((1, 1, TOP_K * tm), lambda i: (i, 0, 0), memory_space=pltpu.SMEM),
                  pl.BlockSpec((1, 1, TOP_K * tm), lambda i: (jnp.minimum(i + 1, last), 0, 0),
                               memory_space=pltpu.SMEM),
                  pl.BlockSpec((tm, D_MODEL), lambda i: (blk0 + i, 0)),
                  pl.BlockSpec((tm, LANES), lambda i: (blk0 + i, 0)),
                  pl.BlockSpec(memory_space=pl.ANY),
                  _const_spec((1, D_MODEL)), _const_spec((1, D_MODEL))],
        out_specs=pl.BlockSpec((tm, D_MODEL), lambda i: (i, 0)),
        out_shape=jax.ShapeDtypeStruct((n, D_MODEL), F32),
        scratch_shapes=[pltpu.VMEM((2, TOP_K * tm * ROW_TILES, LANES), F32),
                        pltpu.SemaphoreType.DMA((2,))],
        compiler_params=_params(("arbitrary",)),
        name="combine",
    )(idx, idx, h1, rec, yb, lng, lnb)


PROMPT_TILE = 512
COMBINE_TILE = 256


def _row(v, width=None):
    v = v.reshape(1, -1).astype(F32)
    if width is not None and v.shape[1] < width:
        v = jnp.pad(v, ((0, 0), (0, width - v.shape[1])))
    return v


def _layer(layer, hp, hs, ck, cv, conv_st, rec_all, pool_st, lw, experts_w):
    w_packed = _pack_w_in(lw['w_in'])
    alog_row = _row(lw['gdn_a_log'], LANES)
    dtb_row = _row(lw['gdn_dt_bias'], LANES)
    norm_g = _row(lw['gdn_norm_g'])
    pool_scale = _row(lw['pool_scale'])
    sinks = lw['attn_sinks'].astype(F32)
    kvw = A_KV_HEADS * A_HEAD_DIM

    gates_p, bqkv_p, aq_p, z_p, cp_p, kv_p, ab_p = _in_proj(hp, w_packed, PROMPT_TILE)
    oa_p = _attn_prompt(sinks, aq_p, kv_p)
    ob_p, rec_p = _gdn_prompt(bqkv_p, z_p, ab_p, lw['gdn_conv_w'], alog_row, dtb_row, norm_g)
    oc_p = _pool_prompt(cp_p, lw['pool_w'], pool_scale)
    kv_tail = kv_p.reshape(BATCH, SEQ, 2, A_KV_HEADS, A_HEAD_DIM)[:, SEQ - WINDOW:]
    new_k_p, new_v_p = kv_tail[:, :, 0], kv_tail[:, :, 1]
    new_conv_p = bqkv_p.reshape(BATCH, SEQ, 3 * B_WIDTH)[:, SEQ - (CONV_W - 1):]
    new_pool_p = cp_p.reshape(BATCH, SEQ, C_WIDTH)[:, SEQ - POOL_HIST:]

    gates_s, bqkv_s, aq_s, z_s, cp_s, kv_s, ab_s = _in_proj(hs, w_packed, DEC_BATCH)
    oa_s3, nk_s, nv_s = _attn_sample(layer, sinks, aq_s.astype(F32).reshape(DEC_BATCH, A_HEADS, A_HEAD_DIM),
                                     kv_s[:, :kvw], kv_s[:, kvw:], ck, cv)
    oa_s = oa_s3.reshape(DEC_BATCH, A_WIDTH).astype(BF16)
    ob_s, nconv_s, rec_s = _gdn_sample(layer, jnp.transpose(conv_st, (1, 0, 2)), bqkv_s, z_s, ab_s, lw['gdn_conv_w'],
                                       alog_row, dtb_row, norm_g, rec_all)
    oc_s, npool_s = _pool_sample(jnp.transpose(pool_st, (1, 0, 2)), cp_s, lw['pool_w'], pool_scale)
    new_k_s = nk_s.reshape(DEC_BATCH, WINDOW, A_KV_HEADS, A_HEAD_DIM)
    new_v_s = nv_s.reshape(DEC_BATCH, WINDOW, A_KV_HEADS, A_HEAD_DIM)
    new_conv_s = jnp.transpose(nconv_s, (1, 0, 2))
    new_pool_s = jnp.transpose(npool_s, (1, 0, 2))

    wa, wb, wc = (lw[k].astype(BF16) for k in ('w_branch_a', 'w_branch_b', 'w_branch_c'))
    wo = lw['w_o'].astype(BF16)
    wr = jnp.concatenate([lw['router_group_w'],
                          jnp.transpose(lw['router_expert_w'], (1, 0, 2)).reshape(D_MODEL, N_EXPERTS),
                          jnp.zeros((D_MODEL, LANES - ROUTER_COLS), F32)], axis=-1)
    br = _row(jnp.concatenate([lw['router_group_b'], lw['router_expert_b'].reshape(-1)]), LANES)
    ln1g, ln1b, ln2g, ln2b = (_row(lw[k]) for k in ('ln1_g', 'ln1_b', 'ln2_g', 'ln2_b'))
    h1, logits = _merge((hp, oa_p, ob_p, oc_p, gates_p), (hs, oa_s, ob_s, oc_s, gates_s),
                        wa, wb, wc, wo, ln1g, ln1b, wr, br, PROMPT_TILE)
    rec, dest, end_padded, padded, blk_exp, n_used = _route(logits)
    xb = _dispatch(end_padded, padded, n_used, dest, h1)
    yb = _experts(layer, blk_exp, n_used, xb, *experts_w)
    h2_p = _combine(dest, rec, h1, yb, ln2g, ln2b, COMBINE_TILE, 0, N_PROMPT)
    h2_s = _combine(dest, rec, h1, yb, ln2g, ln2b, DEC_BATCH, N_PROMPT, DEC_BATCH)
    return (h2_p, h2_s, (new_k_p, new_v_p, new_conv_p, rec_p, new_pool_p),
            (new_k_s, new_v_s, new_conv_s, rec_s, new_pool_s))


def kernel(x_prompt, x_sample, cache_attn_k, cache_attn_v, state_gdn_conv, state_gdn_rec, state_pool, w_in, attn_sinks, gdn_conv_w, gdn_a_log, gdn_dt_bias, gdn_norm_g, pool_w, pool_scale, w_branch_a, w_branch_b, w_branch_c, w_o, ln1_g, ln1_b, router_group_w, router_group_b, router_expert_w, router_expert_b, w_gate, w_up, w_down, ln2_g, ln2_b):
    weights = dict(w_in=w_in, attn_sinks=attn_sinks, gdn_conv_w=gdn_conv_w, gdn_a_log=gdn_a_log,
                   gdn_dt_bias=gdn_dt_bias, gdn_norm_g=gdn_norm_g, pool_w=pool_w, pool_scale=pool_scale,
                   w_branch_a=w_branch_a, w_branch_b=w_branch_b, w_branch_c=w_branch_c, w_o=w_o,
                   ln1_g=ln1_g, ln1_b=ln1_b, router_group_w=router_group_w, router_group_b=router_group_b,
                   router_expert_w=router_expert_w, router_expert_b=router_expert_b,
                   ln2_g=ln2_g, ln2_b=ln2_b)
    hp = x_prompt.reshape(N_PROMPT, D_MODEL)
    hs = x_sample.reshape(DEC_BATCH, D_MODEL)
    kvw = A_KV_HEADS * A_HEAD_DIM
    ck = cache_attn_k.reshape(DEPTH, DEC_BATCH, WINDOW, kvw)
    cv = cache_attn_v.reshape(DEPTH, DEC_BATCH, WINDOW, kvw)
    st_p = [[] for _ in range(5)]
    st_s = [[] for _ in range(5)]
    for l in range(DEPTH):
        lw = {k: v[l] for k, v in weights.items()}
        hp, hs, sp, ss = _layer(l, hp, hs, ck, cv, state_gdn_conv[l], state_gdn_rec, state_pool[l], lw,
                                (w_gate, w_up, w_down))
        for j in range(5):
            st_p[j].append(sp[j])
            st_s[j].append(ss[j])
    return (hp.reshape(BATCH, SEQ, D_MODEL), hs.reshape(DEC_BATCH, 1, D_MODEL),
            *(jnp.stack(t) for t in st_p), *(jnp.stack(t) for t in st_s))
```

```python
import functools

import jax
import jax.numpy as jnp
import numpy as np
from jax import lax
from jax.experimental import pallas as pl
from jax.experimental.pallas import tpu as pltpu

F32 = jnp.float32
BF16 = jnp.bfloat16

D_MODEL = 1024
BATCH = 2
SEQ = 8192
DEPTH = 2
DEC_BATCH = 128
PAST_LEN = 16384
WINDOW = 128
A_HEADS = 8
A_KV_HEADS = 2
A_HEAD_DIM = 64
A_GROUP = A_HEADS // A_KV_HEADS
A_WIDTH = A_HEADS * A_HEAD_DIM
B_HEADS = 4
B_HEAD_DIM = 128
B_WIDTH = B_HEADS * B_HEAD_DIM
CONV_W = 4
GDN_CHUNK = 64
POOL_WINDOWS = (2, 4, 8, 16)
POOL_GROUPS = 4
C_WIDTH = 512
POOL_GROUP_DIM = C_WIDTH // POOL_GROUPS
POOL_HIST = max(POOL_WINDOWS) - 1
N_BRANCHES = 3
IN_SPLIT_SIZES = (A_WIDTH, A_KV_HEADS * A_HEAD_DIM, A_KV_HEADS * A_HEAD_DIM, 3 * B_WIDTH,
                  B_WIDTH, B_HEADS, B_HEADS, C_WIDTH, N_BRANCHES * D_MODEL)
N_GROUPS = 4
EXPERTS_PER_GROUP = 8
N_EXPERTS = N_GROUPS * EXPERTS_PER_GROUP
TOP_K = 2
EXPERT_FF = 512
DN_ALPHA = (2.0 * DEPTH) ** 0.25
LN_EPS = 1e-5
RMS_EPS = 1e-6
NEG_INF = -1e30

LANES = 128
SUBLANES = 8
ROW_TILES = D_MODEL // LANES
VMEM_LIMIT = 56 * 1024 * 1024

N_PROMPT = BATCH * SEQ
N_ROWS = N_PROMPT + DEC_BATCH

SEG_GATES = (0, N_BRANCHES * D_MODEL)
SEG_BQKV = (3072, 3 * B_WIDTH)
SEG_AQ = (4608, A_WIDTH)
SEG_Z = (5120, B_WIDTH)
SEG_CP = (5632, C_WIDTH)
SEG_KV = (6144, 2 * A_KV_HEADS * A_HEAD_DIM)
SEG_AB = (6400, LANES)
IN_PACKED = 6528

MOE_ROWS = 256
MOE_ASSIGN = N_ROWS * TOP_K
MOE_CAP = -(-MOE_ASSIGN // MOE_ROWS) * MOE_ROWS + N_EXPERTS * MOE_ROWS
MOE_BLOCKS = MOE_CAP // MOE_ROWS
ROUTER_COLS = N_GROUPS + N_EXPERTS


def _sigmoid(x):
    return 1.0 / (1.0 + jnp.exp(-x))


def _silu(x):
    return x * _sigmoid(x)


def _dot(a, b):
    return jnp.dot(a, b, preferred_element_type=F32)


def _dot_nt(a, b):
    return lax.dot_general(a, b, (((1,), (1,)), ((), ())), preferred_element_type=F32)


def _dot_tn(a, b):
    return lax.dot_general(a, b, (((0,), (0,)), ((), ())), preferred_element_type=F32)


def _split3(x):
    hi = x.astype(BF16)
    r = x - hi.astype(F32)
    mid = r.astype(BF16)
    lo = (r - mid.astype(F32)).astype(BF16)
    return hi, mid, lo


def _params(sem=None):
    return pltpu.CompilerParams(dimension_semantics=sem, vmem_limit_bytes=VMEM_LIMIT)


def _const_spec(shape):
    nd = len(shape)
    return pl.BlockSpec(shape, lambda *_: (0,) * nd)


def _pack_w_in(w):
    aq, ak, av, bqkv, bz, ba, bb, cp, gate = jnp.split(w, np.cumsum(IN_SPLIT_SIZES)[:-1].tolist(), axis=-1)
    aq = aq * (A_HEAD_DIM ** -0.5)
    ab = jnp.concatenate([ba, bb, jnp.zeros((D_MODEL, LANES - 2 * B_HEADS), w.dtype)], -1)
    return jnp.concatenate([gate, bqkv, aq, bz, cp, ak, av, ab], -1).astype(BF16)


def _in_proj_kernel(x_ref, w_ref, gates_ref, bqkv_ref, aq_ref, z_ref, cp_ref, kv_ref, ab_ref):
    x = x_ref[...].astype(BF16)

    def mm(lo, n):
        return _dot(x, w_ref[:, lo:lo + n])

    step = 512
    for c in range(SEG_GATES[1] // step):
        gates_ref[:, c * step:(c + 1) * step] = _sigmoid(mm(SEG_GATES[0] + c * step, step)).astype(BF16)
    for c in range(SEG_BQKV[1] // step):
        bqkv_ref[:, c * step:(c + 1) * step] = mm(SEG_BQKV[0] + c * step, step)
    aq_ref[...] = mm(*SEG_AQ).astype(BF16)
    z_ref[...] = mm(*SEG_Z)
    cp_ref[...] = mm(*SEG_CP)
    kv_ref[...] = mm(*SEG_KV)
    ab_ref[...] = mm(*SEG_AB)


def _in_proj(x, w_packed, tm):
    n = x.shape[0]
    widths = (SEG_GATES[1], SEG_BQKV[1], SEG_AQ[1], SEG_Z[1], SEG_CP[1], SEG_KV[1], SEG_AB[1])
    dtypes = (BF16, F32, BF16, F32, F32, F32, F32)
    return pl.pallas_call(
        _in_proj_kernel,
        grid=(n // tm,),
        in_specs=[pl.BlockSpec((tm, D_MODEL), lambda i: (i, 0)),
                  pl.BlockSpec((D_MODEL, IN_PACKED), lambda i: (0, 0), pipeline_mode=pl.Buffered(1))],
        out_specs=[pl.BlockSpec((tm, w), lambda i: (i, 0)) for w in widths],
        out_shape=[jax.ShapeDtypeStruct((n, w), dt) for w, dt in zip(widths, dtypes)],
        compiler_params=_params(("arbitrary",)),
        name="in_proj",
    )(x, w_packed)


def _alibi_slope(h):
    return 2.0 ** (-8.0 * (h + 1) / A_HEADS)


def _attn_prompt_kernel(sink_ref, q_ref, kvc_ref, kvp_ref, o_ref):
    n = pl.program_id(1)
    kvc = kvc_ref[...]
    kvp = kvp_ref[...]
    t = lax.broadcasted_iota(jnp.int32, (WINDOW, 2 * WINDOW), 0)
    s = lax.broadcasted_iota(jnp.int32, (WINDOW, 2 * WINDOW), 1)
    dist = WINDOW + t - s
    valid = (dist >= 0) & (dist <= WINDOW) & ((s >= WINDOW) | (n > 0))
    distf = dist.astype(F32)
    hd = A_HEAD_DIM
    vo = A_KV_HEADS * hd
    outs = []
    for kh in range(A_KV_HEADS):
        k = jnp.concatenate([kvp[:, kh * hd:(kh + 1) * hd], kvc[:, kh * hd:(kh + 1) * hd]], axis=0).astype(BF16)
        v = jnp.concatenate([kvp[:, vo + kh * hd:vo + (kh + 1) * hd],
                             kvc[:, vo + kh * hd:vo + (kh + 1) * hd]], axis=0).astype(BF16)
        for g in range(A_GROUP):
            h = kh * A_GROUP + g
            sc = _dot_nt(q_ref[:, h * hd:(h + 1) * hd], k)
            sc = jnp.where(valid, sc - _alibi_slope(h) * distf, NEG_INF)
            sink = sink_ref[h]
            m = jnp.maximum(jnp.max(sc, axis=-1, keepdims=True), sink)
            p = jnp.exp(sc - m)
            denom = jnp.sum(p, axis=-1, keepdims=True) + jnp.exp(sink - m)
            outs.append(_dot(p.astype(BF16), v) / denom)
    o_ref[...] = jnp.concatenate(outs, axis=-1).astype(BF16)


def _attn_prompt(sinks, aq, kv):
    nb = SEQ // WINDOW
    return pl.pallas_call(
        _attn_prompt_kernel,
        grid=(BATCH, nb),
        in_specs=[pl.BlockSpec(memory_space=pltpu.SMEM),
                  pl.BlockSpec((WINDOW, A_WIDTH), lambda b, n: (b * nb + n, 0)),
                  pl.BlockSpec((WINDOW, SEG_KV[1]), lambda b, n: (b * nb + n, 0)),
                  pl.BlockSpec((WINDOW, SEG_KV[1]), lambda b, n: (b * nb + jnp.maximum(n - 1, 0), 0))],
        out_specs=pl.BlockSpec((WINDOW, A_WIDTH), lambda b, n: (b * nb + n, 0)),
        out_shape=jax.ShapeDtypeStruct((N_PROMPT, A_WIDTH), BF16),
        compiler_params=_params(("arbitrary", "arbitrary")),
        name="attn_prompt",
    )(sinks, aq, kv, kv)


ATTN_S_BLOCK = 8


def _attn_sample_kernel(sink_ref, q_ref, kn_ref, vn_ref, ck_ref, cv_ref, o_ref, nk_ref, nv_ref):
    hd = A_HEAD_DIM
    row = lax.broadcasted_iota(jnp.int32, (A_HEADS, 1), 0)
    slope = jnp.zeros((A_HEADS, 1), F32)
    sink = jnp.zeros((A_HEADS, 1), F32)
    for h in range(A_HEADS):
        slope = jnp.where(row == h, _alibi_slope(h), slope)
        sink = jnp.where(row == h, sink_ref[h], sink)
    pos = lax.broadcasted_iota(jnp.int32, (A_HEADS, WINDOW), 1)
    bias = -slope * (WINDOW - pos).astype(F32)
    first = row < A_GROUP
    for b in range(ATTN_S_BLOCK):
        kc = ck_ref[b]
        vc = cv_ref[b]
        qf = q_ref[b]
        q = qf.astype(BF16)
        kn = kn_ref[b:b + 1, :]
        vn = vn_ref[b:b + 1, :]
        s0 = _dot_nt(q, kc[:, 0:hd].astype(BF16))
        s1 = _dot_nt(q, kc[:, hd:2 * hd].astype(BF16))
        sc = jnp.where(first, s0, s1) + bias
        sn0 = jnp.sum(qf * kn[:, 0:hd], axis=-1, keepdims=True)
        sn1 = jnp.sum(qf * kn[:, hd:2 * hd], axis=-1, keepdims=True)
        sn = jnp.where(first, sn0, sn1)
        m = jnp.maximum(jnp.maximum(jnp.max(sc, axis=-1, keepdims=True), sn), sink)
        p = jnp.exp(sc - m)
        pn = jnp.exp(sn - m)
        denom = jnp.sum(p, axis=-1, keepdims=True) + pn + jnp.exp(sink - m)
        pb = p.astype(BF16)
        o0 = _dot(pb, vc[:, 0:hd].astype(BF16)) + pn * vn[:, 0:hd]
        o1 = _dot(pb, vc[:, hd:2 * hd].astype(BF16)) + pn * vn[:, hd:2 * hd]
        o_ref[b] = jnp.where(first, o0, o1) / denom
        nk_ref[b, 0:WINDOW - 1, :] = kc[1:WINDOW, :]
        nk_ref[b, WINDOW - 1:WINDOW, :] = kn
        nv_ref[b, 0:WINDOW - 1, :] = vc[1:WINDOW, :]
        nv_ref[b, WINDOW - 1:WINDOW, :] = vn


def _attn_sample(layer, sinks, q3, kn, vn, ck, cv):
    bs = ATTN_S_BLOCK
    kvw = A_KV_HEADS * A_HEAD_DIM
    return pl.pallas_call(
        _attn_sample_kernel,
        grid=(DEC_BATCH // bs,),
        in_specs=[pl.BlockSpec(memory_space=pltpu.SMEM),
                  pl.BlockSpec((bs, A_HEADS, A_HEAD_DIM), lambda i: (i, 0, 0)),
                  pl.BlockSpec((bs, kvw), lambda i: (i, 0)),
                  pl.BlockSpec((bs, kvw), lambda i: (i, 0)),
                  pl.BlockSpec((None, bs, WINDOW, kvw), lambda i: (layer, i, 0, 0)),
                  pl.BlockSpec((None, bs, WINDOW, kvw), lambda i: (layer, i, 0, 0))],
        out_specs=[pl.BlockSpec((bs, A_HEADS, A_HEAD_DIM), lambda i: (i, 0, 0)),
                   pl.BlockSpec((bs, WINDOW, kvw), lambda i: (i, 0, 0)),
                   pl.BlockSpec((bs, WINDOW, kvw), lambda i: (i, 0, 0))],
        out_shape=[jax.ShapeDtypeStruct((DEC_BATCH, A_HEADS, A_HEAD_DIM), F32),
                   jax.ShapeDtypeStruct((DEC_BATCH, WINDOW, kvw), F32),
                   jax.ShapeDtypeStruct((DEC_BATCH, WINDOW, kvw), F32)],
        compiler_params=_params(("arbitrary",)),
        name="attn_sample",
    )(sinks, q3, kn, vn, ck, cv)


GDN_TILE = 256
GDN_STACK = B_HEADS * GDN_CHUNK


def _gdn_gates(ab, alog, dtb):
    xs = ab + dtb
    softplus = jnp.maximum(xs, 0.0) + jnp.log(1.0 + jnp.exp(-jnp.abs(xs)))
    g = -jnp.exp(alog) * softplus
    beta = _sigmoid(ab)
    return g, beta


def _l2norm(x):
    return x * lax.rsqrt(jnp.sum(x * x, axis=-1, keepdims=True) + RMS_EPS)


def _gdn_out_norm(o, z, norm_g):
    return o * lax.rsqrt(jnp.mean(o * o, axis=-1, keepdims=True) + RMS_EPS) * norm_g * _silu(z)


def _gdn_prompt_kernel(x_ref, z_ref, ab_ref, cw_ref, alog_ref, dtb_ref, ng_ref,
                       o_ref, sfin_ref, xbuf_ref, s_ref):
    n = pl.program_id(1)
    t_rows = GDN_TILE
    c_rows = GDN_CHUNK
    hd = B_HEAD_DIM
    pad = SUBLANES

    @pl.when(n == 0)
    def _():
        xbuf_ref[0:pad, :] = jnp.zeros((pad, 3 * B_WIDTH), F32)
        s_ref[...] = jnp.zeros_like(s_ref)

    x = x_ref[...]
    xbuf_ref[pad:pad + t_rows, :] = x
    cw = cw_ref[...]
    conv = x * cw[CONV_W - 1:CONV_W, :]
    for j in range(CONV_W - 1):
        back = CONV_W - 1 - j
        conv = conv + xbuf_ref[pad - back:pad - back + t_rows, :] * cw[j:j + 1, :]
    xbuf_ref[0:pad, :] = x[t_rows - pad:t_rows, :]
    conv = _silu(conv)

    g_all, beta_all = _gdn_gates(ab_ref[...], alog_ref[...], dtb_ref[...])
    ri = lax.broadcasted_iota(jnp.int32, (t_rows, t_rows), 0)
    ci = lax.broadcasted_iota(jnp.int32, (t_rows, t_rows), 1)
    tri = jnp.where((ri // c_rows == ci // c_rows) & (ri >= ci), 1.0, 0.0).astype(BF16)
    g_hi, g_mid, g_lo = _split3(g_all)
    gcum_all = _dot(tri, g_hi) + _dot(tri, g_mid) + _dot(tri, g_lo)

    qn, kn, vv = [], [], []
    for h in range(B_HEADS):
        qn.append(_l2norm(conv[:, h * hd:(h + 1) * hd]) * (hd ** -0.5))
        kn.append(_l2norm(conv[:, B_WIDTH + h * hd:B_WIDTH + (h + 1) * hd]))
        vv.append(conv[:, 2 * B_WIDTH + h * hd:2 * B_WIDTH + (h + 1) * hd])

    st = GDN_STACK
    nc = t_rows // c_rows
    r = lax.broadcasted_iota(jnp.int32, (st, st), 0)
    c = lax.broadcasted_iota(jnp.int32, (st, st), 1)
    same = (r // c_rows) == (c // c_rows)
    incl = same & (r >= c)
    strict = same & (r > c)

    def chunk_rows(a, ck):
        return a[ck * c_rows:(ck + 1) * c_rows]

    def stack(parts):
        return jnp.stack([jnp.concatenate([chunk_rows(p, ck) for p in parts], axis=0) for ck in range(nc)], axis=0)

    def stack_col(a, lane0):
        return stack([a[:, lane0 + h:lane0 + h + 1] for h in range(B_HEADS)])

    qs, ks, vs = stack(qn), stack(kn), stack(vv)
    gcol = stack_col(gcum_all, 0)
    bcol = stack_col(beta_all, B_HEADS)
    glast = [[gcum_all[(ck + 1) * c_rows - 1:(ck + 1) * c_rows, h:h + 1] for h in range(B_HEADS)] for ck in range(nc)]
    glast_col = jnp.stack([jnp.concatenate([jnp.broadcast_to(gl, (c_rows, 1)) for gl in glast[ck]], axis=0)
                           for ck in range(nc)], axis=0)
    gt_hi, gt_mid, gt_lo = _split3(g_all.T)
    tri_t = jnp.where((ri // c_rows == ci // c_rows) & (ri <= ci), 1.0, 0.0).astype(BF16)
    gcum_t = _dot(gt_hi, tri_t) + _dot(gt_mid, tri_t) + _dot(gt_lo, tri_t)
    grow = jnp.stack([jnp.concatenate([gcum_t[h:h + 1, ck * c_rows:(ck + 1) * c_rows] for h in range(B_HEADS)], axis=1)
                      for ck in range(nc)], axis=0)

    def bmm(a, b):
        return jnp.einsum('cij,cjk->cik', a, b, preferred_element_type=F32)

    def bmm_nt(a, b):
        return jnp.einsum('cik,cjk->cij', a, b, preferred_element_type=F32)

    decay = jnp.where(incl, jnp.exp(jnp.where(incl, gcol - grow, 0.0)), 0.0)
    exp_g = jnp.exp(gcol)
    kb = ks * bcol
    vb = vs * bcol
    ks16 = ks.astype(BF16)
    kk = bmm_nt(kb.astype(BF16), ks16)
    qk = bmm_nt(qs.astype(BF16), ks16)
    m_mat = jnp.where(strict, kk * decay, 0.0)
    m16 = m_mat.astype(BF16)
    zero16 = jnp.zeros((), BF16)
    p16 = jnp.where(r == c, 1.0, 0.0).astype(BF16) - jnp.where((r // 2) == (c // 2), m16, zero16)
    blk = 2
    while blk < c_rows:
        lvl = ((r // (2 * blk)) == (c // (2 * blk))) & ((r // blk) != (c // blk))
        a_off = jnp.where(lvl, m16, zero16)
        p16 = p16 - bmm(bmm(p16, a_off).astype(BF16), p16).astype(BF16)
        blk *= 2
    uw = bmm(p16, jnp.concatenate([vb, kb * exp_g], axis=2).astype(BF16))
    aqk = (qk * decay).astype(BF16)
    qg = qs * exp_g
    kd = (ks * jnp.exp(glast_col - gcol)).astype(BF16)

    o_chunks = []
    for ck in range(nc):
        vnew, qs_s = [], []
        for h in range(B_HEADS):
            hs = slice(h * c_rows, (h + 1) * c_rows)
            wq = jnp.concatenate([uw[ck, hs, hd:2 * hd], qg[ck, hs]], axis=0).astype(BF16)
            rs = _dot(wq, s_ref[h].astype(BF16))
            vnew.append(uw[ck, hs, 0:hd] - rs[0:c_rows])
            qs_s.append(rs[c_rows:2 * c_rows])
        vnew16 = jnp.concatenate(vnew, axis=0).astype(BF16)
        o_st = jnp.concatenate(qs_s, axis=0) + _dot(aqk[ck], vnew16)
        for h in range(B_HEADS):
            hs = slice(h * c_rows, (h + 1) * c_rows)
            s_ref[h] = s_ref[h] * jnp.exp(glast[ck][h]) + _dot_tn(kd[ck, hs], vnew16[hs])
        o_chunks.append([o_st[h * c_rows:(h + 1) * c_rows] for h in range(B_HEADS)])

    ng = ng_ref[...]
    z = z_ref[...]
    outs = []
    for h in range(B_HEADS):
        o_h = jnp.concatenate([oc[h] for oc in o_chunks], axis=0)
        outs.append(_gdn_out_norm(o_h, z[:, h * hd:(h + 1) * hd], ng))
    o_ref[...] = jnp.concatenate(outs, axis=-1).astype(BF16)
    sfin_ref[0] = s_ref[...]


def _gdn_prompt(bqkv, z, ab, conv_w, alog_row, dtb_row, norm_g):
    nt = SEQ // GDN_TILE
    t = GDN_TILE
    return pl.pallas_call(
        _gdn_prompt_kernel,
        grid=(BATCH, nt),
        in_specs=[pl.BlockSpec((t, 3 * B_WIDTH), lambda b, n: (b * nt + n, 0)),
                  pl.BlockSpec((t, B_WIDTH), lambda b, n: (b * nt + n, 0)),
                  pl.BlockSpec((t, LANES), lambda b, n: (b * nt + n, 0)),
                  _const_spec((CONV_W, 3 * B_WIDTH)),
                  _const_spec((1, LANES)),
                  _const_spec((1, LANES)),
                  _const_spec((1, B_HEAD_DIM))],
        out_specs=[pl.BlockSpec((t, B_WIDTH), lambda b, n: (b * nt + n, 0)),
                   pl.BlockSpec((1, B_HEADS, B_HEAD_DIM, B_HEAD_DIM), lambda b, n: (b, 0, 0, 0))],
        out_shape=[jax.ShapeDtypeStruct((N_PROMPT, B_WIDTH), BF16),
                   jax.ShapeDtypeStruct((BATCH, B_HEADS, B_HEAD_DIM, B_HEAD_DIM), F32)],
        scratch_shapes=[pltpu.VMEM((SUBLANES + t, 3 * B_WIDTH), F32),
                        pltpu.VMEM((B_HEADS, B_HEAD_DIM, B_HEAD_DIM), F32)],
        compiler_params=_params(("arbitrary", "arbitrary")),
        name="gdn_prompt",
    )(bqkv, z, ab, conv_w, alog_row, dtb_row, norm_g)


GDN_S_BLOCK = 8


def _gdn_sample_kernel(st_ref, x_ref, z_ref, ab_ref, cw_ref, alog_ref, dtb_ref, ng_ref, s_ref,
                       o_ref, nst_ref, ns_ref):
    hd = B_HEAD_DIM
    x = x_ref[...]
    cw = cw_ref[...]
    conv = x * cw[CONV_W - 1:CONV_W, :]
    for j in range(CONV_W - 1):
        conv = conv + st_ref[j] * cw[j:j + 1, :]
    for j in range(CONV_W - 2):
        nst_ref[j] = st_ref[j + 1]
    nst_ref[CONV_W - 2] = x
    conv = _silu(conv)
    g_all, beta_all = _gdn_gates(ab_ref[...], alog_ref[...], dtb_ref[...])
    eg_all = jnp.exp(g_all)
    r = lax.broadcasted_iota(jnp.int32, (hd, hd), 0)
    c = lax.broadcasted_iota(jnp.int32, (hd, hd), 1)
    eye = r == c
    zpad = jnp.zeros((SUBLANES - 2, hd), F32)
    ng = ng_ref[...]
    z = z_ref[...]
    outs = []
    for h in range(B_HEADS):
        q = _l2norm(conv[:, h * hd:(h + 1) * hd]) * (hd ** -0.5)
        k = _l2norm(conv[:, B_WIDTH + h * hd:B_WIDTH + (h + 1) * hd])
        v = conv[:, 2 * B_WIDTH + h * hd:2 * B_WIDTH + (h + 1) * hd]
        beta = beta_all[:, B_HEADS + h:B_HEADS + h + 1]
        eg = eg_all[:, h:h + 1]
        qk = jnp.sum(q * k, axis=-1, keepdims=True)
        rows = []
        for b in range(GDN_S_BLOCK):
            kb = k[b:b + 1]
            s_bh = s_ref[b, h]
            kq = jnp.concatenate([kb, q[b:b + 1], zpad], axis=0).astype(BF16)
            rs = _dot(kq, s_bh.astype(BF16))
            bb = beta[b:b + 1]
            egb = eg[b:b + 1]
            v_new = bb * v[b:b + 1] - (bb * egb) * rs[0:1]
            rows.append(egb * rs[1:2] + qk[b:b + 1] * v_new)
            kdiag = jnp.where(eye, jnp.broadcast_to(kb, (hd, hd)), 0.0).astype(BF16)
            outer = _dot(kdiag, jnp.broadcast_to(v_new, (hd, hd)).astype(BF16))
            ns_ref[b, h] = s_bh * egb + outer
        o_h = jnp.concatenate(rows, axis=0)
        outs.append(_gdn_out_norm(o_h, z[:, h * hd:(h + 1) * hd], ng))
    o_ref[...] = jnp.concatenate(outs, axis=-1).astype(BF16)


def _gdn_sample(layer, conv_st, bqkv, z, ab, conv_w, alog_row, dtb_row, norm_g, s0):
    bs = GDN_S_BLOCK
    cw3 = 3 * B_WIDTH
    return pl.pallas_call(
        _gdn_sample_kernel,
        grid=(DEC_BATCH // bs,),
        in_specs=[pl.BlockSpec((CONV_W - 1, bs, cw3), lambda i: (0, i, 0)),
                  pl.BlockSpec((bs, cw3), lambda i: (i, 0)),
                  pl.BlockSpec((bs, B_WIDTH), lambda i: (i, 0)),
                  pl.BlockSpec((bs, LANES), lambda i: (i, 0)),
                  _const_spec((CONV_W, cw3)),
                  _const_spec((1, LANES)),
                  _const_spec((1, LANES)),
                  _const_spec((1, B_HEAD_DIM)),
                  pl.BlockSpec((None, bs, B_HEADS, B_HEAD_DIM, B_HEAD_DIM), lambda i: (layer, i, 0, 0, 0))],
        out_specs=[pl.BlockSpec((bs, B_WIDTH), lambda i: (i, 0)),
                   pl.BlockSpec((CONV_W - 1, bs, cw3), lambda i: (0, i, 0)),
                   pl.BlockSpec((bs, B_HEADS, B_HEAD_DIM, B_HEAD_DIM), lambda i: (i, 0, 0, 0))],
        out_shape=[jax.ShapeDtypeStruct((DEC_BATCH, B_WIDTH), BF16),
                   jax.ShapeDtypeStruct((CONV_W - 1, DEC_BATCH, cw3), F32),
                   jax.ShapeDtypeStruct((DEC_BATCH, B_HEADS, B_HEAD_DIM, B_HEAD_DIM), F32)],
        compiler_params=_params(("arbitrary",)),
        name="gdn_sample",
    )(conv_st, bqkv, z, ab, conv_w, alog_row, dtb_row, norm_g, s0)


POOL_TILE = 512
POOL_PAD = 16


def _pool_project(d, pw_ref, scale):
    outs = []
    for gi in range(POOL_GROUPS):
        lo = gi * POOL_GROUP_DIM
        outs.append(_dot(d[:, lo:lo + POOL_GROUP_DIM].astype(BF16), pw_ref[gi].astype(BF16)))
    return jnp.concatenate(outs, axis=-1) * scale


def _pool_prompt_kernel(x_ref, pw_ref, ps_ref, o_ref, buf_ref):
    n = pl.program_id(1)
    t = POOL_TILE
    pad = POOL_PAD
    gd = POOL_GROUP_DIM

    @pl.when(n == 0)
    def _():
        buf_ref[0:pad, :] = jnp.zeros((pad, C_WIDTH), F32)

    x = x_ref[...]
    buf_ref[pad:pad + t, :] = x
    s1 = buf_ref[1:pad + t, :]
    s2 = s1 + buf_ref[0:pad + t - 1, :]
    s4 = s2[2:, gd:] + s2[:-2, gd:]
    s8 = s4[4:, gd:] + s4[:-4, gd:]
    s16 = s8[8:, gd:] + s8[:-8, gd:]
    sums = (s2[15:, 0:gd], s4[13:, 0:gd], s8[9:, 0:gd], s16[1:, :])
    pos = n * t + lax.broadcasted_iota(jnp.int32, (t, 1), 0) + 1
    means = []
    for gi, w in enumerate(POOL_WINDOWS):
        cnt = jnp.minimum(pos, w).astype(F32)
        means.append(sums[gi] / cnt)
    d = jnp.concatenate(means, axis=-1) - x
    o_ref[...] = _pool_project(d, pw_ref, ps_ref[...]).astype(BF16)
    buf_ref[0:pad, :] = x[t - pad:t, :]


def _pool_prompt(cp, pool_w, pool_scale):
    nt = SEQ // POOL_TILE
    t = POOL_TILE
    return pl.pallas_call(
        _pool_prompt_kernel,
        grid=(BATCH, nt),
        in_specs=[pl.BlockSpec((t, C_WIDTH), lambda b, n: (b * nt + n, 0)),
                  _const_spec((POOL_GROUPS, POOL_GROUP_DIM, POOL_GROUP_DIM)),
                  _const_spec((1, C_WIDTH))],
        out_specs=pl.BlockSpec((t, C_WIDTH), lambda b, n: (b * nt + n, 0)),
        out_shape=jax.ShapeDtypeStruct((N_PROMPT, C_WIDTH), BF16),
        scratch_shapes=[pltpu.VMEM((POOL_PAD + t, C_WIDTH), F32)],
        compiler_params=_params(("arbitrary", "arbitrary")),
        name="pool_prompt",
    )(cp, pool_w, pool_scale)


def _pool_sample_kernel(st_ref, x_ref, pw_ref, ps_ref, o_ref, nst_ref):
    x = x_ref[...]
    gd = POOL_GROUP_DIM
    means = []
    for gi, w in enumerate(POOL_WINDOWS):
        lo = gi * gd
        acc = x[:, lo:lo + gd]
        for j in range(1, w):
            acc = acc + st_ref[POOL_HIST - j][:, lo:lo + gd]
        means.append(acc / float(min(PAST_LEN + 1, w)))
    d = jnp.concatenate(means, axis=-1) - x
    o_ref[...] = _pool_project(d, pw_ref, ps_ref[...]).astype(BF16)
    for j in range(POOL_HIST - 1):
        nst_ref[j] = st_ref[j + 1]
    nst_ref[POOL_HIST - 1] = x


def _pool_sample(pool_st, cp, pool_w, pool_scale):
    return pl.pallas_call(
        _pool_sample_kernel,
        out_shape=[jax.ShapeDtypeStruct((DEC_BATCH, C_WIDTH), BF16),
                   jax.ShapeDtypeStruct((POOL_HIST, DEC_BATCH, C_WIDTH), F32)],
        compiler_params=_params(),
        name="pool_sample",
    )(pool_st, cp, pool_w, pool_scale)


def _layer_norm(y, g, b):
    mu = jnp.mean(y, axis=-1, keepdims=True)
    yc = y - mu
    var = jnp.mean(yc * yc, axis=-1, keepdims=True)
    return yc * lax.rsqrt(var + LN_EPS) * g + b


def _store_token_tiled(ref, val, rows):
    for s in range(ROW_TILES):
        ref[pl.ds(s, rows, stride=ROW_TILES), :] = val[:, s * LANES:(s + 1) * LANES]


def _load_token_tiled(ref, base, rows):
    return jnp.concatenate([ref[pl.ds(base + s, rows, stride=ROW_TILES), :] for s in range(ROW_TILES)], axis=-1)


def _merge_kernel(hp_ref, oap_ref, obp_ref, ocp_ref, gp_ref, hs_ref, oas_ref, obs_ref, ocs_ref, gs_ref,
                  wa_ref, wb_ref, wc_ref, wo_ref, lng_ref, lnb_ref, wr_ref, br_ref,
                  h1_ref, lg_ref, *, n_prompt_tiles):
    i = pl.program_id(0)

    def run(h_ref, oa_ref, ob_ref, oc_ref, g_ref):
        rows = h_ref.shape[0]
        merged = (g_ref[:, 0:D_MODEL].astype(F32) * _dot(oa_ref[...], wa_ref[...])
                  + g_ref[:, D_MODEL:2 * D_MODEL].astype(F32) * _dot(ob_ref[...], wb_ref[...])
                  + g_ref[:, 2 * D_MODEL:3 * D_MODEL].astype(F32) * _dot(oc_ref[...], wc_ref[...]))
        mix = _dot(merged.astype(BF16), wo_ref[...])
        h1 = _layer_norm(DN_ALPHA * h_ref[...] + mix, lng_ref[...], lnb_ref[...])
        h1_ref[0:rows, :] = h1
        hh, hm, _ = _split3(h1)
        wh, wm, _ = _split3(wr_ref[...])
        lg_ref[0:rows, :] = _dot(hh, wh) + _dot(hh, wm) + _dot(hm, wh) + br_ref[...]

    @pl.when(i < n_prompt_tiles)
    def _():
        run(hp_ref, oap_ref, obp_ref, ocp_ref, gp_ref)

    @pl.when(i == n_prompt_tiles)
    def _():
        run(hs_ref, oas_ref, obs_ref, ocs_ref, gs_ref)


def _merge(prompt, sample, wa, wb, wc, wo, lng, lnb, wr, br, tm):
    n_p = N_PROMPT // tm
    last = n_p - 1
    widths = (D_MODEL, A_WIDTH, B_WIDTH, C_WIDTH, N_BRANCHES * D_MODEL)
    in_specs = ([pl.BlockSpec((tm, w), lambda i: (jnp.minimum(i, last), 0)) for w in widths]
                + [_const_spec((DEC_BATCH, w)) for w in widths]
                + [_const_spec((A_WIDTH, D_MODEL)), _const_spec((B_WIDTH, D_MODEL)), _const_spec((C_WIDTH, D_MODEL)),
                   _const_spec((D_MODEL, D_MODEL)), _const_spec((1, D_MODEL)), _const_spec((1, D_MODEL)),
                   _const_spec((D_MODEL, LANES)), _const_spec((1, LANES))])
    return pl.pallas_call(
        functools.partial(_merge_kernel, n_prompt_tiles=n_p),
        grid=(n_p + 1,),
        in_specs=in_specs,
        out_specs=[pl.BlockSpec((tm, D_MODEL), lambda i: (i, 0)),
                   pl.BlockSpec((tm, LANES), lambda i: (i, 0))],
        out_shape=[jax.ShapeDtypeStruct((N_ROWS, D_MODEL), F32),
                   jax.ShapeDtypeStruct((N_ROWS, LANES), F32)],
        compiler_params=_params(("arbitrary",)),
        name="merge",
    )(*prompt, *sample, wa, wb, wc, wo, lng, lnb, wr, br)


ROUTE_TILE = 384
REC_EXPERT = 0
REC_RANK = 2
REC_WEIGHT = 4


def _route_kernel(lg_ref, rec_ref, sizes_ref, carry_ref):
    i = pl.program_id(0)
    t_rows = ROUTE_TILE

    @pl.when(i == 0)
    def _():
        carry_ref[...] = jnp.zeros_like(carry_ref)

    lg = lg_ref[...]
    lane = lax.broadcasted_iota(jnp.int32, (t_rows, LANES), 1).astype(F32)
    far = float(LANES)

    def masked_argmax(mask):
        v = jnp.max(jnp.where(mask, lg, NEG_INF), axis=-1, keepdims=True)
        idx = jnp.min(jnp.where(mask & (lg == v), lane, far), axis=-1, keepdims=True)
        return v, idx

    gmask = lane < N_GROUPS
    gmax, gsel = masked_argmax(gmask)
    gw = 1.0 / jnp.sum(jnp.where(gmask, jnp.exp(lg - gmax), 0.0), axis=-1, keepdims=True)
    lo = N_GROUPS + EXPERTS_PER_GROUP * gsel
    emask = (lane >= lo) & (lane < lo + EXPERTS_PER_GROUP)
    v1, i1 = masked_argmax(emask)
    v2, i2 = masked_argmax(emask & (lane != i1))
    t = jnp.exp(v2 - v1)
    w1 = gw / (1.0 + t)
    w2 = gw * t / (1.0 + t)
    e1 = i1 - N_GROUPS
    e2 = i2 - N_GROUPS

    o1 = jnp.where(lane == e1, 1.0, 0.0)
    o2 = jnp.where(lane == e2, 1.0, 0.0)
    r = lax.broadcasted_iota(jnp.int32, (t_rows, t_rows), 0)
    c = lax.broadcasted_iota(jnp.int32, (t_rows, t_rows), 1)
    before = jnp.where(c < r, 1.0, 0.0).astype(BF16)
    carry = carry_ref[...]
    tot1 = jnp.sum(o1, axis=0, keepdims=True)
    tot2 = jnp.sum(o2, axis=0, keepdims=True)
    rank1 = jnp.sum(o1 * (_dot(before, o1.astype(BF16)) + carry), axis=-1, keepdims=True)
    rank2 = jnp.sum(o2 * (_dot(before, o2.astype(BF16)) + (carry + tot1)), axis=-1, keepdims=True)
    carry = carry + tot1 + tot2
    carry_ref[...] = carry
    sizes_ref[...] = carry

    rec = jnp.zeros((t_rows, LANES), F32)
    for k, val in enumerate((e1, e2, rank1, rank2, w1, w2)):
        rec = jnp.where(lane == k, val, rec)
    rec_ref[...] = rec


def _route(logits):
    rec, sizes_row = pl.pallas_call(
        _route_kernel,
        grid=(N_ROWS // ROUTE_TILE,),
        in_specs=[pl.BlockSpec((ROUTE_TILE, LANES), lambda i: (i, 0))],
        out_specs=[pl.BlockSpec((ROUTE_TILE, LANES), lambda i: (i, 0)), _const_spec((1, LANES))],
        out_shape=[jax.ShapeDtypeStruct((N_ROWS, LANES), F32), jax.ShapeDtypeStruct((1, LANES), F32)],
        scratch_shapes=[pltpu.VMEM((1, LANES), F32)],
        compiler_params=_params(("arbitrary",)),
        name="route",
    )(logits)
    eid = rec[:, REC_EXPERT:REC_EXPERT + TOP_K].astype(jnp.int32)
    rank = rec[:, REC_RANK:REC_RANK + TOP_K].astype(jnp.int32)
    sizes = sizes_row[0, :N_EXPERTS].astype(jnp.int32)
    padded = (sizes + MOE_ROWS - 1) // MOE_ROWS * MOE_ROWS
    end_padded = jnp.cumsum(padded).astype(jnp.int32)
    start_padded = end_padded - padded
    experts = jnp.arange(N_EXPERTS, dtype=jnp.int32)
    dest = jnp.sum(jnp.where(eid[:, :, None] == experts, start_padded, 0), axis=-1) + rank
    blk_start = jnp.arange(MOE_BLOCKS, dtype=jnp.int32) * MOE_ROWS
    n_used = end_padded[-1] // MOE_ROWS
    blk_exp = jnp.sum((blk_start[:, None] >= end_padded[None, :]).astype(jnp.int32), axis=-1)
    last_exp = jnp.sum((blk_start[jnp.maximum(n_used - 1, 0)] >= end_padded).astype(jnp.int32))
    blk_exp = jnp.where(blk_start < end_padded[-1], blk_exp, last_exp).astype(jnp.int32)
    return rec, dest, end_padded, padded.astype(jnp.int32), blk_exp, n_used.reshape(1).astype(jnp.int32)


def _tile_rows(ref, j):
    return ref.at[pl.ds(pl.multiple_of(j * ROW_TILES, ROW_TILES), ROW_TILES), :]


def _gather_rows(idx_ref, src_ref, dst_ref, sem, count):
    def body(jj, carry):
        for p in range(2):
            j = 2 * jj + p
            pltpu.make_async_copy(_tile_rows(src_ref, idx_ref[0, 0, j]), _tile_rows(dst_ref, j), sem).start(priority=p)
        return carry
    lax.fori_loop(0, count // 2, body, 0)


def _gather_wait(src_ref, dst_ref, sem, count):
    pltpu.make_async_copy(src_ref.at[pl.ds(0, count * ROW_TILES), :], dst_ref, sem).wait()


DISPATCH_TILE = 384


def _dispatch_kernel(endp_ref, padded_ref, nused_ref, idx_ref, h_ref, xb_hbm, stage_ref, zero_ref, sem_ref, zsem_ref):
    i = pl.program_id(0)
    n = pl.num_programs(0)
    tm = h_ref.shape[0]
    slot = i % 2
    blk = MOE_ROWS * ROW_TILES

    def zero_copy(b0):
        return pltpu.make_async_copy(zero_ref, xb_hbm.at[pl.ds(pl.multiple_of(b0 * ROW_TILES, ROW_TILES), blk), :], zsem_ref.at[0])

    @pl.when(i == 0)
    def _():
        zero_ref[...] = jnp.zeros_like(zero_ref)
        for e in range(N_EXPERTS):
            @pl.when(padded_ref[e] > 0)
            def _():
                zero_copy(endp_ref[e] - MOE_ROWS).start()

        def tail_start(b, carry):
            zero_copy(b * MOE_ROWS).start()
            return carry
        lax.fori_loop(nused_ref[0], MOE_BLOCKS, tail_start, 0)
        for e in range(N_EXPERTS):
            @pl.when(padded_ref[e] > 0)
            def _():
                zero_copy(endp_ref[e] - MOE_ROWS).wait()

        def tail_wait(b, carry):
            zero_copy(b * MOE_ROWS).wait()
            return carry
        lax.fori_loop(nused_ref[0], MOE_BLOCKS, tail_wait, 0)

    def wait_slot(s):
        for _ in range(TOP_K):
            pltpu.make_async_copy(stage_ref.at[s], xb_hbm.at[pl.ds(0, tm * ROW_TILES), :], sem_ref.at[s]).wait()

    @pl.when(i >= 2)
    def _():
        wait_slot(slot)

    stage = stage_ref.at[slot]
    _store_token_tiled(stage, h_ref[...], tm)

    def body(t, carry):
        for k in range(TOP_K):
            pltpu.make_async_copy(_tile_rows(stage, t), _tile_rows(xb_hbm, idx_ref[0, 0, k * tm + t]),
                                  sem_ref.at[slot]).start(priority=k)
        return carry
    lax.fori_loop(0, tm, body, 0)

    @pl.when(i == n - 1)
    def _():
        wait_slot(slot)

        @pl.when(n > 1)
        def _():
            wait_slot(1 - slot)


def _tile_major_idx(dest, n_tiles, tm):
    idx = dest.reshape(n_tiles, tm, TOP_K)
    return jnp.transpose(idx, (0, 2, 1)).reshape(n_tiles, 1, TOP_K * tm)


def _dispatch(end_padded, padded, n_used, dest, h1):
    tm = DISPATCH_TILE
    n_tiles = N_ROWS // tm
    idx = _tile_major_idx(dest, n_tiles, tm)
    grid_spec = pltpu.PrefetchScalarGridSpec(
        num_scalar_prefetch=3,
        grid=(n_tiles,),
        in_specs=[pl.BlockSpec((1, 1, TOP_K * tm), lambda i, *_: (i, 0, 0), memory_space=pltpu.SMEM),
                  pl.BlockSpec((tm, D_MODEL), lambda i, *_: (i, 0))],
        out_specs=pl.BlockSpec(memory_space=pl.ANY),
        scratch_shapes=[pltpu.VMEM((2, tm * ROW_TILES, LANES), F32),
                        pltpu.VMEM((MOE_ROWS * ROW_TILES, LANES), F32),
                        pltpu.SemaphoreType.DMA((2,)),
                        pltpu.SemaphoreType.DMA((1,))])
    return pl.pallas_call(
        _dispatch_kernel,
        grid_spec=grid_spec,
        out_shape=jax.ShapeDtypeStruct((MOE_CAP * ROW_TILES, LANES), F32),
        compiler_params=_params(("arbitrary",)),
        name="dispatch",
    )(end_padded, padded, n_used, idx, h1)


def _expert_kernel(bexp_ref, nused_ref, x_ref, wg_ref, wu_ref, wd_ref, y_ref, wg16_ref, wu16_ref, wd16_ref):
    i = pl.program_id(0)
    n_used = nused_ref[0]
    rows = MOE_ROWS

    @pl.when((i == 0) | (bexp_ref[i] != bexp_ref[jnp.maximum(i - 1, 0)]))
    def _():
        wg16_ref[...] = wg_ref[...].astype(BF16)
        wu16_ref[...] = wu_ref[...].astype(BF16)
        wd16_ref[...] = wd_ref[...].astype(BF16)

    @pl.when(i < n_used)
    def _():
        x = _load_token_tiled(x_ref, 0, rows).astype(BF16)
        hid = _silu(_dot(x, wg16_ref[...])) * _dot(x, wu16_ref[...])
        _store_token_tiled(y_ref, _dot(hid.astype(BF16), wd16_ref[...]), rows)

    @pl.when(i >= n_used)
    def _():
        y_ref[...] = jnp.zeros_like(y_ref)


def _experts(layer, blk_exp, n_used, xb, w_gate, w_up, w_down):
    rows = MOE_ROWS
    blk = (rows * ROW_TILES, LANES)
    grid_spec = pltpu.PrefetchScalarGridSpec(
        num_scalar_prefetch=2,
        grid=(MOE_BLOCKS,),
        in_specs=[pl.BlockSpec(blk, lambda i, be, nu: (jnp.minimum(i, nu[0] - 1), 0)),
                  pl.BlockSpec((None, None, D_MODEL, EXPERT_FF), lambda i, be, nu: (layer, be[i], 0, 0)),
                  pl.BlockSpec((None, None, D_MODEL, EXPERT_FF), lambda i, be, nu: (layer, be[i], 0, 0)),
                  pl.BlockSpec((None, None, EXPERT_FF, D_MODEL), lambda i, be, nu: (layer, be[i], 0, 0))],
        out_specs=pl.BlockSpec(blk, lambda i, be, nu: (i, 0)),
        scratch_shapes=[pltpu.VMEM((D_MODEL, EXPERT_FF), BF16),
                        pltpu.VMEM((D_MODEL, EXPERT_FF), BF16),
                        pltpu.VMEM((EXPERT_FF, D_MODEL), BF16)])
    return pl.pallas_call(
        _expert_kernel,
        grid_spec=grid_spec,
        out_shape=jax.ShapeDtypeStruct((MOE_CAP * ROW_TILES, LANES), F32),
        compiler_params=_params(("arbitrary",)),
        name="experts",
    )(blk_exp, n_used, xb, w_gate, w_up, w_down)


def _combine_kernel(idx_ref, idxn_ref, h1_ref, rec_ref, y_hbm, lng_ref, lnb_ref, o_ref, ybuf_ref, sem_ref, *, n_tiles):
    i = pl.program_id(0)
    tm = h1_ref.shape[0]
    slot = i % 2
    count = TOP_K * tm

    @pl.when(i == 0)
    def _():
        _gather_rows(idx_ref, y_hbm, ybuf_ref.at[0], sem_ref.at[0], count)

    @pl.when(i + 1 < n_tiles)
    def _():
        _gather_rows(idxn_ref, y_hbm, ybuf_ref.at[1 - slot], sem_ref.at[1 - slot], count)

    _gather_wait(y_hbm, ybuf_ref.at[slot], sem_ref.at[slot], count)
    y = DN_ALPHA * h1_ref[...]
    for k in range(TOP_K):
        wk = rec_ref[:, REC_WEIGHT + k:REC_WEIGHT + k + 1]
        y = y + wk * _load_token_tiled(ybuf_ref.at[slot], k * tm * ROW_TILES, tm)
    o_ref[...] = _layer_norm(y, lng_ref[...], lnb_ref[...])


def _combine(dest, rec, h1, yb, lng, lnb, tm, row0, n):
    n_tiles = n // tm
    blk0 = row0 // tm
    idx = _tile_major_idx(dest[row0:row0 + n], n_tiles, tm)
    last = n_tiles - 1
    return pl.pallas_call(
        functools.partial(_combine_kernel, n_tiles=n_tiles),
        grid=(n_tiles,),
        in_specs=[pl.BlockSpec((1, 1, TOP_K * tm), lambda i: (i, 0, 0), memory_space=pltpu.SMEM),
                  pl.BlockSpec((1, 1, TOP_K * tm), lambda i: (jnp.minimum(i + 1, last), 0, 0),
                               memory_space=pltpu.SMEM),
                  pl.BlockSpec((tm, D_MODEL), lambda i: (blk0 + i, 0)),
                  pl.BlockSpec((tm, LANES), lambda i: (blk0 + i, 0)),
                  pl.BlockSpec(memory_space=pl.ANY),
                  _const_spec((1, D_MODEL)), _const_spec((1, D_MODEL))],
        out_specs=pl.BlockSpec((tm, D_MODEL), lambda i: (i, 0)),
        out_shape=jax.ShapeDtypeStruct((n, D_MODEL), F32),
        scratch_shapes=[pltpu.VMEM((2, TOP_K * tm * ROW_TILES, LANES), F32),
                        pltpu.SemaphoreType.DMA((2,))],
        compiler_params=_params(("arbitrary",)),
        name="combine",
    )(idx, idx, h1, rec, yb, lng, lnb)


PROMPT_TILE = 512
COMBINE_TILE = 256


def _row(v, width=None):
    v = v.reshape(1, -1).astype(F32)
    if width is not None and v.shape[1] < width:
        v = jnp.pad(v, ((0, 0), (0, width - v.shape[1])))
    return v


def _layer(layer, hp, hs, ck, cv, conv_st, rec_all, pool_st, lw, experts_w):
    w_packed = _pack_w_in(lw['w_in'])
    alog_row = _row(lw['gdn_a_log'], LANES)
    dtb_row = _row(lw['gdn_dt_bias'], LANES)
    norm_g = _row(lw['gdn_norm_g'])
    pool_scale = _row(lw['pool_scale'])
    sinks = lw['attn_sinks'].astype(F32)
    kvw = A_KV_HEADS * A_HEAD_DIM

    gates_p, bqkv_p, aq_p, z_p, cp_p, kv_p, ab_p = _in_proj(hp, w_packed, PROMPT_TILE)
    oa_p = _attn_prompt(sinks, aq_p, kv_p)
    ob_p, rec_p = _gdn_prompt(bqkv_p, z_p, ab_p, lw['gdn_conv_w'], alog_row, dtb_row, norm_g)
    oc_p = _pool_prompt(cp_p, lw['pool_w'], pool_scale)
    kv_tail = kv_p.reshape(BATCH, SEQ, 2, A_KV_HEADS, A_HEAD_DIM)[:, SEQ - WINDOW:]
    new_k_p, new_v_p = kv_tail[:, :, 0], kv_tail[:, :, 1]
    new_conv_p = bqkv_p.reshape(BATCH, SEQ, 3 * B_WIDTH)[:, SEQ - (CONV_W - 1):]
    new_pool_p = cp_p.reshape(BATCH, SEQ, C_WIDTH)[:, SEQ - POOL_HIST:]

    gates_s, bqkv_s, aq_s, z_s, cp_s, kv_s, ab_s = _in_proj(hs, w_packed, DEC_BATCH)
    oa_s3, nk_s, nv_s = _attn_sample(layer, sinks, aq_s.astype(F32).reshape(DEC_BATCH, A_HEADS, A_HEAD_DIM),
                                     kv_s[:, :kvw], kv_s[:, kvw:], ck, cv)
    oa_s = oa_s3.reshape(DEC_BATCH, A_WIDTH).astype(BF16)
    ob_s, nconv_s, rec_s = _gdn_sample(layer, jnp.transpose(conv_st, (1, 0, 2)), bqkv_s, z_s, ab_s, lw['gdn_conv_w'],
                                       alog_row, dtb_row, norm_g, rec_all)
    oc_s, npool_s = _pool_sample(jnp.transpose(pool_st, (1, 0, 2)), cp_s, lw['pool_w'], pool_scale)
    new_k_s = nk_s.reshape(DEC_BATCH, WINDOW, A_KV_HEADS, A_HEAD_DIM)
    new_v_s = nv_s.reshape(DEC_BATCH, WINDOW, A_KV_HEADS, A_HEAD_DIM)
    new_conv_s = jnp.transpose(nconv_s, (1, 0, 2))
    new_pool_s = jnp.transpose(npool_s, (1, 0, 2))

    wa, wb, wc = (lw[k].astype(BF16) for k in ('w_branch_a', 'w_branch_b', 'w_branch_c'))
    wo = lw['w_o'].astype(BF16)
    wr = jnp.concatenate([lw['router_group_w'],
                          jnp.transpose(lw['router_expert_w'], (1, 0, 2)).reshape(D_MODEL, N_EXPERTS),
                          jnp.zeros((D_MODEL, LANES - ROUTER_COLS), F32)], axis=-1)
    br = _row(jnp.concatenate([lw['router_group_b'], lw['router_expert_b'].reshape(-1)]), LANES)
    ln1g, ln1b, ln2g, ln2b = (_row(lw[k]) for k in ('ln1_g', 'ln1_b', 'ln2_g', 'ln2_b'))
    h1, logits = _merge((hp, oa_p, ob_p, oc_p, gates_p), (hs, oa_s, ob_s, oc_s, gates_s),
                        wa, wb, wc, wo, ln1g, ln1b, wr, br, PROMPT_TILE)
    rec, dest, end_padded, padded, blk_exp, n_used = _route(logits)
    xb = _dispatch(end_padded, padded, n_used, dest, h1)
    yb = _experts(layer, blk_exp, n_used, xb, *experts_w)
    h2_p = _combine(dest, rec, h1, yb, ln2g, ln2b, COMBINE_TILE, 0, N_PROMPT)
    h2_s = _combine(dest, rec, h1, yb, ln2g, ln2b, DEC_BATCH, N_PROMPT, DEC_BATCH)
    return (h2_p, h2_s, (new_k_p, new_v_p, new_conv_p, rec_p, new_pool_p),
            (new_k_s, new_v_s, new_conv_s, rec_s, new_pool_s))


def kernel(x_prompt, x_sample, cache_attn_k, cache_attn_v, state_gdn_conv, state_gdn_rec, state_pool, w_in, attn_sinks, gdn_conv_w, gdn_a_log, gdn_dt_bias, gdn_norm_g, pool_w, pool_scale, w_branch_a, w_branch_b, w_branch_c, w_o, ln1_g, ln1_b, router_group_w, router_group_b, router_expert_w, router_expert_b, w_gate, w_up, w_down, ln2_g, ln2_b):
    weights = dict(w_in=w_in, attn_sinks=attn_sinks, gdn_conv_w=gdn_conv_w, gdn_a_log=gdn_a_log,
                   gdn_dt_bias=gdn_dt_bias, gdn_norm_g=gdn_norm_g, pool_w=pool_w, pool_scale=pool_scale,
                   w_branch_a=w_branch_a, w_branch_b=w_branch_b, w_branch_c=w_branch_c, w_o=w_o,
                   ln1_g=ln1_g, ln1_b=ln1_b, router_group_w=router_group_w, router_group_b=router_group_b,
                   router_expert_w=router_expert_w, router_expert_b=router_expert_b,
                   ln2_g=ln2_g, ln2_b=ln2_b)
    hp = x_prompt.reshape(N_PROMPT, D_MODEL)
    hs = x_sample.reshape(DEC_BATCH, D_MODEL)
    kvw = A_KV_HEADS * A_HEAD_DIM
    ck = cache_attn_k.reshape(DEPTH, DEC_BATCH, WINDOW, kvw)
    cv = cache_attn_v.reshape(DEPTH, DEC_BATCH, WINDOW, kvw)
    st_p = [[] for _ in range(5)]
    st_s = [[] for _ in range(5)]
    for l in range(DEPTH):
        lw = {k: v[l] for k, v in weights.items()}
        hp, hs, sp, ss = _layer(l, hp, hs, ck, cv, state_gdn_conv[l], state_gdn_rec, state_pool[l], lw,
                                (w_gate, w_up, w_down))
        for j in range(5):
            st_p[j].append(sp[j])
            st_s[j].append(ss[j])
    return (hp.reshape(BATCH, SEQ, D_MODEL), hs.reshape(DEC_BATCH, 1, D_MODEL),
            *(jnp.stack(t) for t in st_p), *(jnp.stack(t) for t in st_s))
```

```python
import functools

import jax
import jax.numpy as jnp
import numpy as np
from jax import lax
from jax.experimental import pallas as pl
from jax.experimental.pallas import tpu as pltpu

F32 = jnp.float32
BF16 = jnp.bfloat16

D_MODEL = 1024
BATCH = 2
SEQ = 8192
DEPTH = 2
DEC_BATCH = 128
PAST_LEN = 16384
WINDOW = 128
A_HEADS = 8
A_KV_HEADS = 2
A_HEAD_DIM = 64
A_GROUP = A_HEADS // A_KV_HEADS
A_WIDTH = A_HEADS * A_HEAD_DIM
B_HEADS = 4
B_HEAD_DIM = 128
B_WIDTH = B_HEADS * B_HEAD_DIM
CONV_W = 4
GDN_CHUNK = 64
POOL_WINDOWS = (2, 4, 8, 16)
POOL_GROUPS = 4
C_WIDTH = 512
POOL_GROUP_DIM = C_WIDTH // POOL_GROUPS
POOL_HIST = max(POOL_WINDOWS) - 1
N_BRANCHES = 3
IN_SPLIT_SIZES = (A_WIDTH, A_KV_HEADS * A_HEAD_DIM, A_KV_HEADS * A_HEAD_DIM, 3 * B_WIDTH,
                  B_WIDTH, B_HEADS, B_HEADS, C_WIDTH, N_BRANCHES * D_MODEL)
N_GROUPS = 4
EXPERTS_PER_GROUP = 8
N_EXPERTS = N_GROUPS * EXPERTS_PER_GROUP
TOP_K = 2
EXPERT_FF = 512
DN_ALPHA = (2.0 * DEPTH) ** 0.25
LN_EPS = 1e-5
RMS_EPS = 1e-6
NEG_INF = -1e30

LANES = 128
SUBLANES = 8
ROW_TILES = D_MODEL // LANES
VMEM_LIMIT = 56 * 1024 * 1024

N_PROMPT = BATCH * SEQ
N_ROWS = N_PROMPT + DEC_BATCH

SEG_GATES = (0, N_BRANCHES * D_MODEL)
SEG_BQKV = (3072, 3 * B_WIDTH)
SEG_AQ = (4608, A_WIDTH)
SEG_Z = (5120, B_WIDTH)
SEG_CP = (5632, C_WIDTH)
SEG_KV = (6144, 2 * A_KV_HEADS * A_HEAD_DIM)
SEG_AB = (6400, LANES)
IN_PACKED = 6528

MOE_ROWS = 256
MOE_ASSIGN = N_ROWS * TOP_K
MOE_CAP = -(-MOE_ASSIGN // MOE_ROWS) * MOE_ROWS + N_EXPERTS * MOE_ROWS
MOE_BLOCKS = MOE_CAP // MOE_ROWS
ROUTER_COLS = N_GROUPS + N_EXPERTS


def _sigmoid(x):
    return 1.0 / (1.0 + jnp.exp(-x))


def _silu(x):
    return x * _sigmoid(x)


def _dot(a, b):
    return jnp.dot(a, b, preferred_element_type=F32)


def _dot_nt(a, b):
    return lax.dot_general(a, b, (((1,), (1,)), ((), ())), preferred_element_type=F32)


def _dot_tn(a, b):
    return lax.dot_general(a, b, (((0,), (0,)), ((), ())), preferred_element_type=F32)


def _split3(x):
    hi = x.astype(BF16)
    r = x - hi.astype(F32)
    mid = r.astype(BF16)
    lo = (r - mid.astype(F32)).astype(BF16)
    return hi, mid, lo


def _params(sem=None):
    return pltpu.CompilerParams(dimension_semantics=sem, vmem_limit_bytes=VMEM_LIMIT)


def _const_spec(shape):
    nd = len(shape)
    return pl.BlockSpec(shape, lambda *_: (0,) * nd)


def _pack_w_in(w):
    aq, ak, av, bqkv, bz, ba, bb, cp, gate = jnp.split(w, np.cumsum(IN_SPLIT_SIZES)[:-1].tolist(), axis=-1)
    aq = aq * (A_HEAD_DIM ** -0.5)
    ab = jnp.concatenate([ba, bb, jnp.zeros(w.shape[:-1] + (LANES - 2 * B_HEADS,), w.dtype)], -1)
    return jnp.concatenate([gate, bqkv, aq, bz, cp, ak, av, ab], -1).astype(BF16)


def _in_proj_kernel(x_ref, w_ref, gates_ref, bqkv_ref, aq_ref, z_ref, cp_ref, kv_ref, ab_ref):
    x = x_ref[...].astype(BF16)

    def mm(lo, n):
        return _dot(x, w_ref[:, lo:lo + n])

    step = 512
    for c in range(SEG_GATES[1] // step):
        gates_ref[:, c * step:(c + 1) * step] = _sigmoid(mm(SEG_GATES[0] + c * step, step)).astype(BF16)
    for c in range(SEG_BQKV[1] // step):
        bqkv_ref[:, c * step:(c + 1) * step] = mm(SEG_BQKV[0] + c * step, step)
    aq_ref[...] = mm(*SEG_AQ).astype(BF16)
    z_ref[...] = mm(*SEG_Z)
    cp_ref[...] = mm(*SEG_CP)
    kv_ref[...] = mm(*SEG_KV)
    ab_ref[...] = mm(*SEG_AB)


def _in_proj(layer, x, w_packed, tm):
    n = x.shape[0]
    widths = (SEG_GATES[1], SEG_BQKV[1], SEG_AQ[1], SEG_Z[1], SEG_CP[1], SEG_KV[1], SEG_AB[1])
    dtypes = (BF16, F32, BF16, F32, F32, F32, F32)
    return pl.pallas_call(
        _in_proj_kernel,
        grid=(n // tm,),
        in_specs=[pl.BlockSpec((tm, D_MODEL), lambda i: (i, 0)),
                  pl.BlockSpec((None, D_MODEL, IN_PACKED), lambda i: (layer, 0, 0), pipeline_mode=pl.Buffered(1))],
        out_specs=[pl.BlockSpec((tm, w), lambda i: (i, 0)) for w in widths],
        out_shape=[jax.ShapeDtypeStruct((n, w), dt) for w, dt in zip(widths, dtypes)],
        compiler_params=_params(("arbitrary",)),
        name="in_proj",
    )(x, w_packed)


def _alibi_slope(h):
    return 2.0 ** (-8.0 * (h + 1) / A_HEADS)


def _attn_prompt_kernel(sink_ref, q_ref, kvc_ref, kvp_ref, o_ref):
    n = pl.program_id(1)
    kvc = kvc_ref[...]
    kvp = kvp_ref[...]
    t = lax.broadcasted_iota(jnp.int32, (WINDOW, 2 * WINDOW), 0)
    s = lax.broadcasted_iota(jnp.int32, (WINDOW, 2 * WINDOW), 1)
    dist = WINDOW + t - s
    valid = (dist >= 0) & (dist <= WINDOW) & ((s >= WINDOW) | (n > 0))
    distf = dist.astype(F32)
    hd = A_HEAD_DIM
    vo = A_KV_HEADS * hd
    outs = []
    for kh in range(A_KV_HEADS):
        k = jnp.concatenate([kvp[:, kh * hd:(kh + 1) * hd], kvc[:, kh * hd:(kh + 1) * hd]], axis=0).astype(BF16)
        v = jnp.concatenate([kvp[:, vo + kh * hd:vo + (kh + 1) * hd],
                             kvc[:, vo + kh * hd:vo + (kh + 1) * hd]], axis=0).astype(BF16)
        for g in range(A_GROUP):
            h = kh * A_GROUP + g
            sc = _dot_nt(q_ref[:, h * hd:(h + 1) * hd], k)
            sc = jnp.where(valid, sc - _alibi_slope(h) * distf, NEG_INF)
            sink = sink_ref[h]
            m = jnp.maximum(jnp.max(sc, axis=-1, keepdims=True), sink)
            p = jnp.exp(sc - m)
            denom = jnp.sum(p, axis=-1, keepdims=True) + jnp.exp(sink - m)
            outs.append(_dot(p.astype(BF16), v) / denom)
    o_ref[...] = jnp.concatenate(outs, axis=-1).astype(BF16)


def _attn_prompt(sinks, aq, kv):
    nb = SEQ // WINDOW
    return pl.pallas_call(
        _attn_prompt_kernel,
        grid=(BATCH, nb),
        in_specs=[pl.BlockSpec(memory_space=pltpu.SMEM),
                  pl.BlockSpec((WINDOW, A_WIDTH), lambda b, n: (b * nb + n, 0)),
                  pl.BlockSpec((WINDOW, SEG_KV[1]), lambda b, n: (b * nb + n, 0)),
                  pl.BlockSpec((WINDOW, SEG_KV[1]), lambda b, n: (b * nb + jnp.maximum(n - 1, 0), 0))],
        out_specs=pl.BlockSpec((WINDOW, A_WIDTH), lambda b, n: (b * nb + n, 0)),
        out_shape=jax.ShapeDtypeStruct((N_PROMPT, A_WIDTH), BF16),
        compiler_params=_params(("arbitrary", "arbitrary")),
        name="attn_prompt",
    )(sinks, aq, kv, kv)


ATTN_S_BLOCK = 8


def _attn_sample_kernel(sink_ref, q_ref, kn_ref, vn_ref, ck_ref, cv_ref, o_ref, nk_ref, nv_ref):
    hd = A_HEAD_DIM
    row = lax.broadcasted_iota(jnp.int32, (A_HEADS, 1), 0)
    slope = jnp.zeros((A_HEADS, 1), F32)
    sink = jnp.zeros((A_HEADS, 1), F32)
    for h in range(A_HEADS):
        slope = jnp.where(row == h, _alibi_slope(h), slope)
        sink = jnp.where(row == h, sink_ref[h], sink)
    pos = lax.broadcasted_iota(jnp.int32, (A_HEADS, WINDOW), 1)
    bias = -slope * (WINDOW - pos).astype(F32)
    first = row < A_GROUP
    for b in range(ATTN_S_BLOCK):
        kc = ck_ref[b]
        vc = cv_ref[b]
        qf = q_ref[b]
        q = qf.astype(BF16)
        kn = kn_ref[b:b + 1, :]
        vn = vn_ref[b:b + 1, :]
        s0 = _dot_nt(q, kc[:, 0:hd].astype(BF16))
        s1 = _dot_nt(q, kc[:, hd:2 * hd].astype(BF16))
        sc = jnp.where(first, s0, s1) + bias
        sn0 = jnp.sum(qf * kn[:, 0:hd], axis=-1, keepdims=True)
        sn1 = jnp.sum(qf * kn[:, hd:2 * hd], axis=-1, keepdims=True)
        sn = jnp.where(first, sn0, sn1)
        m = jnp.maximum(jnp.maximum(jnp.max(sc, axis=-1, keepdims=True), sn), sink)
        p = jnp.exp(sc - m)
        pn = jnp.exp(sn - m)
        denom = jnp.sum(p, axis=-1, keepdims=True) + pn + jnp.exp(sink - m)
        pb = p.astype(BF16)
        o0 = _dot(pb, vc[:, 0:hd].astype(BF16)) + pn * vn[:, 0:hd]
        o1 = _dot(pb, vc[:, hd:2 * hd].astype(BF16)) + pn * vn[:, hd:2 * hd]
        o_ref[b] = jnp.where(first, o0, o1) / denom
        nk_ref[b, 0:WINDOW - 1, :] = kc[1:WINDOW, :]
        nk_ref[b, WINDOW - 1:WINDOW, :] = kn
        nv_ref[b, 0:WINDOW - 1, :] = vc[1:WINDOW, :]
        nv_ref[b, WINDOW - 1:WINDOW, :] = vn


def _attn_sample(layer, sinks, q3, kn, vn, ck, cv):
    bs = ATTN_S_BLOCK
    kvw = A_KV_HEADS * A_HEAD_DIM
    return pl.pallas_call(
        _attn_sample_kernel,
        grid=(DEC_BATCH // bs,),
        in_specs=[pl.BlockSpec(memory_space=pltpu.SMEM),
                  pl.BlockSpec((bs, A_HEADS, A_HEAD_DIM), lambda i: (i, 0, 0)),
                  pl.BlockSpec((bs, kvw), lambda i: (i, 0)),
                  pl.BlockSpec((bs, kvw), lambda i: (i, 0)),
                  pl.BlockSpec((None, bs, WINDOW, kvw), lambda i: (layer, i, 0, 0)),
                  pl.BlockSpec((None, bs, WINDOW, kvw), lambda i: (layer, i, 0, 0))],
        out_specs=[pl.BlockSpec((bs, A_HEADS, A_HEAD_DIM), lambda i: (i, 0, 0)),
                   pl.BlockSpec((bs, WINDOW, kvw), lambda i: (i, 0, 0)),
                   pl.BlockSpec((bs, WINDOW, kvw), lambda i: (i, 0, 0))],
        out_shape=[jax.ShapeDtypeStruct((DEC_BATCH, A_HEADS, A_HEAD_DIM), F32),
                   jax.ShapeDtypeStruct((DEC_BATCH, WINDOW, kvw), F32),
                   jax.ShapeDtypeStruct((DEC_BATCH, WINDOW, kvw), F32)],
        compiler_params=_params(("arbitrary",)),
        name="attn_sample",
    )(sinks, q3, kn, vn, ck, cv)


GDN_TILE = 256
GDN_STACK = B_HEADS * GDN_CHUNK


def _gdn_gates(ab, alog, dtb):
    xs = ab + dtb
    softplus = jnp.maximum(xs, 0.0) + jnp.log(1.0 + jnp.exp(-jnp.abs(xs)))
    g = -jnp.exp(alog) * softplus
    beta = _sigmoid(ab)
    return g, beta


def _l2norm(x):
    return x * lax.rsqrt(jnp.sum(x * x, axis=-1, keepdims=True) + RMS_EPS)


def _gdn_out_norm(o, z, norm_g):
    return o * lax.rsqrt(jnp.mean(o * o, axis=-1, keepdims=True) + RMS_EPS) * norm_g * _silu(z)


def _gdn_prompt_kernel(x_ref, z_ref, ab_ref, cw_ref, alog_ref, dtb_ref, ng_ref,
                       o_ref, sfin_ref, xbuf_ref, s_ref):
    n = pl.program_id(1)
    t_rows = GDN_TILE
    c_rows = GDN_CHUNK
    hd = B_HEAD_DIM
    pad = SUBLANES

    @pl.when(n == 0)
    def _():
        xbuf_ref[0:pad, :] = jnp.zeros((pad, 3 * B_WIDTH), F32)
        s_ref[...] = jnp.zeros_like(s_ref)

    x = x_ref[...]
    xbuf_ref[pad:pad + t_rows, :] = x
    cw = cw_ref[...]
    conv = x * cw[CONV_W - 1:CONV_W, :]
    for j in range(CONV_W - 1):
        back = CONV_W - 1 - j
        conv = conv + xbuf_ref[pad - back:pad - back + t_rows, :] * cw[j:j + 1, :]
    xbuf_ref[0:pad, :] = x[t_rows - pad:t_rows, :]
    conv = _silu(conv)

    g_all, beta_all = _gdn_gates(ab_ref[...], alog_ref[...], dtb_ref[...])
    ri = lax.broadcasted_iota(jnp.int32, (t_rows, t_rows), 0)
    ci = lax.broadcasted_iota(jnp.int32, (t_rows, t_rows), 1)
    tri = jnp.where((ri // c_rows == ci // c_rows) & (ri >= ci), 1.0, 0.0).astype(BF16)
    g_hi, g_mid, g_lo = _split3(g_all)
    gcum_all = _dot(tri, g_hi) + _dot(tri, g_mid) + _dot(tri, g_lo)

    qn, kn, vv = [], [], []
    for h in range(B_HEADS):
        qn.append(_l2norm(conv[:, h * hd:(h + 1) * hd]) * (hd ** -0.5))
        kn.append(_l2norm(conv[:, B_WIDTH + h * hd:B_WIDTH + (h + 1) * hd]))
        vv.append(conv[:, 2 * B_WIDTH + h * hd:2 * B_WIDTH + (h + 1) * hd])

    st = GDN_STACK
    nc = t_rows // c_rows
    r = lax.broadcasted_iota(jnp.int32, (st, st), 0)
    c = lax.broadcasted_iota(jnp.int32, (st, st), 1)
    same = (r // c_rows) == (c // c_rows)
    incl = same & (r >= c)
    strict = same & (r > c)

    def chunk_rows(a, ck):
        return a[ck * c_rows:(ck + 1) * c_rows]

    def stack(parts):
        return jnp.stack([jnp.concatenate([chunk_rows(p, ck) for p in parts], axis=0) for ck in range(nc)], axis=0)

    def stack_col(a, lane0):
        return stack([a[:, lane0 + h:lane0 + h + 1] for h in range(B_HEADS)])

    qs, ks, vs = stack(qn), stack(kn), stack(vv)
    gcol = stack_col(gcum_all, 0)
    bcol = stack_col(beta_all, B_HEADS)
    glast = [[gcum_all[(ck + 1) * c_rows - 1:(ck + 1) * c_rows, h:h + 1] for h in range(B_HEADS)] for ck in range(nc)]
    glast_col = jnp.stack([jnp.concatenate([jnp.broadcast_to(gl, (c_rows, 1)) for gl in glast[ck]], axis=0)
                           for ck in range(nc)], axis=0)
    gt_hi, gt_mid, gt_lo = _split3(g_all.T)
    tri_t = jnp.where((ri // c_rows == ci // c_rows) & (ri <= ci), 1.0, 0.0).astype(BF16)
    gcum_t = _dot(gt_hi, tri_t) + _dot(gt_mid, tri_t) + _dot(gt_lo, tri_t)
    grow = jnp.stack([jnp.concatenate([gcum_t[h:h + 1, ck * c_rows:(ck + 1) * c_rows] for h in range(B_HEADS)], axis=1)
                      for ck in range(nc)], axis=0)

    def bmm(a, b):
        return jnp.einsum('cij,cjk->cik', a, b, preferred_element_type=F32)

    def bmm_nt(a, b):
        return jnp.einsum('cik,cjk->cij', a, b, preferred_element_type=F32)

    decay = jnp.where(incl, jnp.exp(jnp.where(incl, gcol - grow, 0.0)), 0.0)
    exp_g = jnp.exp(gcol)
    kb = ks * bcol
    vb = vs * bcol
    ks16 = ks.astype(BF16)
    kk = bmm_nt(kb.astype(BF16), ks16)
    qk = bmm_nt(qs.astype(BF16), ks16)
    m_mat = jnp.where(strict, kk * decay, 0.0)
    m16 = m_mat.astype(BF16)
    zero16 = jnp.zeros((), BF16)
    p16 = jnp.where(r == c, 1.0, 0.0).astype(BF16) - jnp.where((r // 2) == (c // 2), m16, zero16)
    blk = 2
    while blk < c_rows:
        lvl = ((r // (2 * blk)) == (c // (2 * blk))) & ((r // blk) != (c // blk))
        a_off = jnp.where(lvl, m16, zero16)
        p16 = p16 - bmm(bmm(p16, a_off).astype(BF16), p16).astype(BF16)
        blk *= 2
    uw = bmm(p16, jnp.concatenate([vb, kb * exp_g], axis=2).astype(BF16))
    aqk = (qk * decay).astype(BF16)
    qg = qs * exp_g
    kd = (ks * jnp.exp(glast_col - gcol)).astype(BF16)

    o_chunks = []
    for ck in range(nc):
        vnew, qs_s = [], []
        for h in range(B_HEADS):
            hs = slice(h * c_rows, (h + 1) * c_rows)
            wq = jnp.concatenate([uw[ck, hs, hd:2 * hd], qg[ck, hs]], axis=0).astype(BF16)
            rs = _dot(wq, s_ref[h].astype(BF16))
            vnew.append(uw[ck, hs, 0:hd] - rs[0:c_rows])
            qs_s.append(rs[c_rows:2 * c_rows])
        vnew16 = jnp.concatenate(vnew, axis=0).astype(BF16)
        o_st = jnp.concatenate(qs_s, axis=0) + _dot(aqk[ck], vnew16)
        for h in range(B_HEADS):
            hs = slice(h * c_rows, (h + 1) * c_rows)
            s_ref[h] = s_ref[h] * jnp.exp(glast[ck][h]) + _dot_tn(kd[ck, hs], vnew16[hs])
        o_chunks.append([o_st[h * c_rows:(h + 1) * c_rows] for h in range(B_HEADS)])

    ng = ng_ref[...]
    z = z_ref[...]
    outs = []
    for h in range(B_HEADS):
        o_h = jnp.concatenate([oc[h] for oc in o_chunks], axis=0)
        outs.append(_gdn_out_norm(o_h, z[:, h * hd:(h + 1) * hd], ng))
    o_ref[...] = jnp.concatenate(outs, axis=-1).astype(BF16)
    sfin_ref[0] = s_ref[...]


def _gdn_prompt(bqkv, z, ab, conv_w, alog_row, dtb_row, norm_g):
    nt = SEQ // GDN_TILE
    t = GDN_TILE
    return pl.pallas_call(
        _gdn_prompt_kernel,
        grid=(BATCH, nt),
        in_specs=[pl.BlockSpec((t, 3 * B_WIDTH), lambda b, n: (b * nt + n, 0)),
                  pl.BlockSpec((t, B_WIDTH), lambda b, n: (b * nt + n, 0)),
                  pl.BlockSpec((t, LANES), lambda b, n: (b * nt + n, 0)),
                  _const_spec((CONV_W, 3 * B_WIDTH)),
                  _const_spec((1, LANES)),
                  _const_spec((1, LANES)),
                  _const_spec((1, B_HEAD_DIM))],
        out_specs=[pl.BlockSpec((t, B_WIDTH), lambda b, n: (b * nt + n, 0)),
                   pl.BlockSpec((1, B_HEADS, B_HEAD_DIM, B_HEAD_DIM), lambda b, n: (b, 0, 0, 0))],
        out_shape=[jax.ShapeDtypeStruct((N_PROMPT, B_WIDTH), BF16),
                   jax.ShapeDtypeStruct((BATCH, B_HEADS, B_HEAD_DIM, B_HEAD_DIM), F32)],
        scratch_shapes=[pltpu.VMEM((SUBLANES + t, 3 * B_WIDTH), F32),
                        pltpu.VMEM((B_HEADS, B_HEAD_DIM, B_HEAD_DIM), F32)],
        compiler_params=_params(("arbitrary", "arbitrary")),
        name="gdn_prompt",
    )(bqkv, z, ab, conv_w, alog_row, dtb_row, norm_g)


GDN_S_BLOCK = 8


def _gdn_sample_kernel(st_ref, x_ref, z_ref, ab_ref, cw_ref, alog_ref, dtb_ref, ng_ref, s_ref,
                       o_ref, nst_ref, ns_ref):
    hd = B_HEAD_DIM
    x = x_ref[...]
    cw = cw_ref[...]
    conv = x * cw[CONV_W - 1:CONV_W, :]
    for j in range(CONV_W - 1):
        conv = conv + st_ref[j] * cw[j:j + 1, :]
    for j in range(CONV_W - 2):
        nst_ref[j] = st_ref[j + 1]
    nst_ref[CONV_W - 2] = x
    conv = _silu(conv)
    g_all, beta_all = _gdn_gates(ab_ref[...], alog_ref[...], dtb_ref[...])
    eg_all = jnp.exp(g_all)
    r = lax.broadcasted_iota(jnp.int32, (hd, hd), 0)
    c = lax.broadcasted_iota(jnp.int32, (hd, hd), 1)
    eye = r == c
    zpad = jnp.zeros((SUBLANES - 2, hd), F32)
    ng = ng_ref[...]
    z = z_ref[...]
    outs = []
    for h in range(B_HEADS):
        q = _l2norm(conv[:, h * hd:(h + 1) * hd]) * (hd ** -0.5)
        k = _l2norm(conv[:, B_WIDTH + h * hd:B_WIDTH + (h + 1) * hd])
        v = conv[:, 2 * B_WIDTH + h * hd:2 * B_WIDTH + (h + 1) * hd]
        beta = beta_all[:, B_HEADS + h:B_HEADS + h + 1]
        eg = eg_all[:, h:h + 1]
        qk = jnp.sum(q * k, axis=-1, keepdims=True)
        rows = []
        for b in range(GDN_S_BLOCK):
            kb = k[b:b + 1]
            s_bh = s_ref[b, h]
            kq = jnp.concatenate([kb, q[b:b + 1], zpad], axis=0).astype(BF16)
            rs = _dot(kq, s_bh.astype(BF16))
            bb = beta[b:b + 1]
            egb = eg[b:b + 1]
            v_new = bb * v[b:b + 1] - (bb * egb) * rs[0:1]
            rows.append(egb * rs[1:2] + qk[b:b + 1] * v_new)
            kdiag = jnp.where(eye, jnp.broadcast_to(kb, (hd, hd)), 0.0).astype(BF16)
            outer = _dot(kdiag, jnp.broadcast_to(v_new, (hd, hd)).astype(BF16))
            ns_ref[b, h] = s_bh * egb + outer
        o_h = jnp.concatenate(rows, axis=0)
        outs.append(_gdn_out_norm(o_h, z[:, h * hd:(h + 1) * hd], ng))
    o_ref[...] = jnp.concatenate(outs, axis=-1).astype(BF16)


def _gdn_sample(layer, conv_st, bqkv, z, ab, conv_w, alog_row, dtb_row, norm_g, s0):
    bs = GDN_S_BLOCK
    cw3 = 3 * B_WIDTH
    return pl.pallas_call(
        _gdn_sample_kernel,
        grid=(DEC_BATCH // bs,),
        in_specs=[pl.BlockSpec((CONV_W - 1, bs, cw3), lambda i: (0, i, 0)),
                  pl.BlockSpec((bs, cw3), lambda i: (i, 0)),
                  pl.BlockSpec((bs, B_WIDTH), lambda i: (i, 0)),
                  pl.BlockSpec((bs, LANES), lambda i: (i, 0)),
                  _const_spec((CONV_W, cw3)),
                  _const_spec((1, LANES)),
                  _const_spec((1, LANES)),
                  _const_spec((1, B_HEAD_DIM)),
                  pl.BlockSpec((None, bs, B_HEADS, B_HEAD_DIM, B_HEAD_DIM), lambda i: (layer, i, 0, 0, 0))],
        out_specs=[pl.BlockSpec((bs, B_WIDTH), lambda i: (i, 0)),
                   pl.BlockSpec((CONV_W - 1, bs, cw3), lambda i: (0, i, 0)),
                   pl.BlockSpec((bs, B_HEADS, B_HEAD_DIM, B_HEAD_DIM), lambda i: (i, 0, 0, 0))],
        out_shape=[jax.ShapeDtypeStruct((DEC_BATCH, B_WIDTH), BF16),
                   jax.ShapeDtypeStruct((CONV_W - 1, DEC_BATCH, cw3), F32),
                   jax.ShapeDtypeStruct((DEC_BATCH, B_HEADS, B_HEAD_DIM, B_HEAD_DIM), F32)],
        compiler_params=_params(("arbitrary",)),
        name="gdn_sample",
    )(conv_st, bqkv, z, ab, conv_w, alog_row, dtb_row, norm_g, s0)


POOL_TILE = 512
POOL_PAD = 16


def _pool_project(d, pw_ref, scale):
    outs = []
    for gi in range(POOL_GROUPS):
        lo = gi * POOL_GROUP_DIM
        outs.append(_dot(d[:, lo:lo + POOL_GROUP_DIM].astype(BF16), pw_ref[gi].astype(BF16)))
    return jnp.concatenate(outs, axis=-1) * scale


def _pool_prompt_kernel(x_ref, pw_ref, ps_ref, o_ref, buf_ref):
    n = pl.program_id(1)
    t = POOL_TILE
    pad = POOL_PAD
    gd = POOL_GROUP_DIM

    @pl.when(n == 0)
    def _():
        buf_ref[0:pad, :] = jnp.zeros((pad, C_WIDTH), F32)

    x = x_ref[...]
    buf_ref[pad:pad + t, :] = x
    s1 = buf_ref[1:pad + t, :]
    s2 = s1 + buf_ref[0:pad + t - 1, :]
    s4 = s2[2:, gd:] + s2[:-2, gd:]
    s8 = s4[4:, gd:] + s4[:-4, gd:]
    s16 = s8[8:, gd:] + s8[:-8, gd:]
    sums = (s2[15:, 0:gd], s4[13:, 0:gd], s8[9:, 0:gd], s16[1:, :])
    pos = n * t + lax.broadcasted_iota(jnp.int32, (t, 1), 0) + 1
    means = []
    for gi, w in enumerate(POOL_WINDOWS):
        cnt = jnp.minimum(pos, w).astype(F32)
        means.append(sums[gi] / cnt)
    d = jnp.concatenate(means, axis=-1) - x
    o_ref[...] = _pool_project(d, pw_ref, ps_ref[...]).astype(BF16)
    buf_ref[0:pad, :] = x[t - pad:t, :]


def _pool_prompt(cp, pool_w, pool_scale):
    nt = SEQ // POOL_TILE
    t = POOL_TILE
    return pl.pallas_call(
        _pool_prompt_kernel,
        grid=(BATCH, nt),
        in_specs=[pl.BlockSpec((t, C_WIDTH), lambda b, n: (b * nt + n, 0)),
                  _const_spec((POOL_GROUPS, POOL_GROUP_DIM, POOL_GROUP_DIM)),
                  _const_spec((1, C_WIDTH))],
        out_specs=pl.BlockSpec((t, C_WIDTH), lambda b, n: (b * nt + n, 0)),
        out_shape=jax.ShapeDtypeStruct((N_PROMPT, C_WIDTH), BF16),
        scratch_shapes=[pltpu.VMEM((POOL_PAD + t, C_WIDTH), F32)],
        compiler_params=_params(("arbitrary", "arbitrary")),
        name="pool_prompt",
    )(cp, pool_w, pool_scale)


def _pool_sample_kernel(st_ref, x_ref, pw_ref, ps_ref, o_ref, nst_ref):
    x = x_ref[...]
    gd = POOL_GROUP_DIM
    means = []
    for gi, w in enumerate(POOL_WINDOWS):
        lo = gi * gd
        acc = x[:, lo:lo + gd]
        for j in range(1, w):
            acc = acc + st_ref[POOL_HIST - j][:, lo:lo + gd]
        means.append(acc / float(min(PAST_LEN + 1, w)))
    d = jnp.concatenate(means, axis=-1) - x
    o_ref[...] = _pool_project(d, pw_ref, ps_ref[...]).astype(BF16)
    for j in range(POOL_HIST - 1):
        nst_ref[j] = st_ref[j + 1]
    nst_ref[POOL_HIST - 1] = x


def _pool_sample(pool_st, cp, pool_w, pool_scale):
    return pl.pallas_call(
        _pool_sample_kernel,
        out_shape=[jax.ShapeDtypeStruct((DEC_BATCH, C_WIDTH), BF16),
                   jax.ShapeDtypeStruct((POOL_HIST, DEC_BATCH, C_WIDTH), F32)],
        compiler_params=_params(),
        name="pool_sample",
    )(pool_st, cp, pool_w, pool_scale)


def _layer_norm(y, g, b):
    mu = jnp.mean(y, axis=-1, keepdims=True)
    yc = y - mu
    var = jnp.mean(yc * yc, axis=-1, keepdims=True)
    return yc * lax.rsqrt(var + LN_EPS) * g + b


def _store_token_tiled(ref, val, rows):
    for s in range(ROW_TILES):
        ref[pl.ds(s, rows, stride=ROW_TILES), :] = val[:, s * LANES:(s + 1) * LANES]


def _load_token_tiled(ref, base, rows):
    return jnp.concatenate([ref[pl.ds(base + s, rows, stride=ROW_TILES), :] for s in range(ROW_TILES)], axis=-1)


def _merge_kernel(hp_ref, oap_ref, obp_ref, ocp_ref, gp_ref, hs_ref, oas_ref, obs_ref, ocs_ref, gs_ref,
                  wa_ref, wb_ref, wc_ref, wo_ref, lng_ref, lnb_ref, wr_ref, br_ref,
                  h1_ref, lg_ref, *, n_prompt_tiles):
    i = pl.program_id(0)

    def run(h_ref, oa_ref, ob_ref, oc_ref, g_ref):
        rows = h_ref.shape[0]
        merged = (g_ref[:, 0:D_MODEL].astype(F32) * _dot(oa_ref[...], wa_ref[...])
                  + g_ref[:, D_MODEL:2 * D_MODEL].astype(F32) * _dot(ob_ref[...], wb_ref[...])
                  + g_ref[:, 2 * D_MODEL:3 * D_MODEL].astype(F32) * _dot(oc_ref[...], wc_ref[...]))
        mix = _dot(merged.astype(BF16), wo_ref[...])
        h1 = _layer_norm(DN_ALPHA * h_ref[...] + mix, lng_ref[...], lnb_ref[...])
        h1_ref[0:rows, :] = h1
        hh, hm, _ = _split3(h1)
        wh, wm, _ = _split3(wr_ref[...])
        lg_ref[0:rows, :] = _dot(hh, wh) + _dot(hh, wm) + _dot(hm, wh) + br_ref[...]

    @pl.when(i < n_prompt_tiles)
    def _():
        run(hp_ref, oap_ref, obp_ref, ocp_ref, gp_ref)

    @pl.when(i == n_prompt_tiles)
    def _():
        run(hs_ref, oas_ref, obs_ref, ocs_ref, gs_ref)


def _merge(prompt, sample, wa, wb, wc, wo, lng, lnb, wr, br, tm):
    n_p = N_PROMPT // tm
    last = n_p - 1
    widths = (D_MODEL, A_WIDTH, B_WIDTH, C_WIDTH, N_BRANCHES * D_MODEL)
    in_specs = ([pl.BlockSpec((tm, w), lambda i: (jnp.minimum(i, last), 0)) for w in widths]
                + [_const_spec((DEC_BATCH, w)) for w in widths]
                + [_const_spec((A_WIDTH, D_MODEL)), _const_spec((B_WIDTH, D_MODEL)), _const_spec((C_WIDTH, D_MODEL)),
                   _const_spec((D_MODEL, D_MODEL)), _const_spec((1, D_MODEL)), _const_spec((1, D_MODEL)),
                   _const_spec((D_MODEL, LANES)), _const_spec((1, LANES))])
    return pl.pallas_call(
        functools.partial(_merge_kernel, n_prompt_tiles=n_p),
        grid=(n_p + 1,),
        in_specs=in_specs,
        out_specs=[pl.BlockSpec((tm, D_MODEL), lambda i: (i, 0)),
                   pl.BlockSpec((tm, LANES), lambda i: (i, 0))],
        out_shape=[jax.ShapeDtypeStruct((N_ROWS, D_MODEL), F32),
                   jax.ShapeDtypeStruct((N_ROWS, LANES), F32)],
        compiler_params=_params(("arbitrary",)),
        name="merge",
    )(*prompt, *sample, wa, wb, wc, wo, lng, lnb, wr, br)


ROUTE_TILE = 384
REC_EXPERT = 0
REC_RANK = 2
REC_WEIGHT = 4


def _route_kernel(lg_ref, rec_ref, sizes_ref, carry_ref):
    i = pl.program_id(0)
    t_rows = ROUTE_TILE

    @pl.when(i == 0)
    def _():
        carry_ref[...] = jnp.zeros_like(carry_ref)

    lg = lg_ref[...]
    lane = lax.broadcasted_iota(jnp.int32, (t_rows, LANES), 1).astype(F32)
    far = float(LANES)

    def masked_argmax(mask):
        v = jnp.max(jnp.where(mask, lg, NEG_INF), axis=-1, keepdims=True)
        idx = jnp.min(jnp.where(mask & (lg == v), lane, far), axis=-1, keepdims=True)
        return v, idx

    gmask = lane < N_GROUPS
    gmax, gsel = masked_argmax(gmask)
    gw = 1.0 / jnp.sum(jnp.where(gmask, jnp.exp(lg - gmax), 0.0), axis=-1, keepdims=True)
    lo = N_GROUPS + EXPERTS_PER_GROUP * gsel
    emask = (lane >= lo) & (lane < lo + EXPERTS_PER_GROUP)
    v1, i1 = masked_argmax(emask)
    v2, i2 = masked_argmax(emask & (lane != i1))
    t = jnp.exp(v2 - v1)
    w1 = gw / (1.0 + t)
    w2 = gw * t / (1.0 + t)
    e1 = i1 - N_GROUPS
    e2 = i2 - N_GROUPS

    o1 = jnp.where(lane == e1, 1.0, 0.0)
    o2 = jnp.where(lane == e2, 1.0, 0.0)
    r = lax.broadcasted_iota(jnp.int32, (t_rows, t_rows), 0)
    c = lax.broadcasted_iota(jnp.int32, (t_rows, t_rows), 1)
    before = jnp.where(c < r, 1.0, 0.0).astype(BF16)
    carry = carry_ref[...]
    tot1 = jnp.sum(o1, axis=0, keepdims=True)
    tot2 = jnp.sum(o2, axis=0, keepdims=True)
    rank1 = jnp.sum(o1 * (_dot(before, o1.astype(BF16)) + carry), axis=-1, keepdims=True)
    rank2 = jnp.sum(o2 * (_dot(before, o2.astype(BF16)) + (carry + tot1)), axis=-1, keepdims=True)
    carry = carry + tot1 + tot2
    carry_ref[...] = carry
    sizes_ref[...] = carry

    rec = jnp.zeros((t_rows, LANES), F32)
    for k, val in enumerate((e1, e2, rank1, rank2, w1, w2)):
        rec = jnp.where(lane == k, val, rec)
    rec_ref[...] = rec


def _route(logits):
    rec, sizes_row = pl.pallas_call(
        _route_kernel,
        grid=(N_ROWS // ROUTE_TILE,),
        in_specs=[pl.BlockSpec((ROUTE_TILE, LANES), lambda i: (i, 0))],
        out_specs=[pl.BlockSpec((ROUTE_TILE, LANES), lambda i: (i, 0)), _const_spec((1, LANES))],
        out_shape=[jax.ShapeDtypeStruct((N_ROWS, LANES), F32), jax.ShapeDtypeStruct((1, LANES), F32)],
        scratch_shapes=[pltpu.VMEM((1, LANES), F32)],
        compiler_params=_params(("arbitrary",)),
        name="route",
    )(logits)
    eid = rec[:, REC_EXPERT:REC_EXPERT + TOP_K].astype(jnp.int32)
    rank = rec[:, REC_RANK:REC_RANK + TOP_K].astype(jnp.int32)
    sizes = sizes_row[0, :N_EXPERTS].astype(jnp.int32)
    padded = (sizes + MOE_ROWS - 1) // MOE_ROWS * MOE_ROWS
    end_padded = jnp.cumsum(padded).astype(jnp.int32)
    start_padded = end_padded - padded
    experts = jnp.arange(N_EXPERTS, dtype=jnp.int32)
    dest = jnp.sum(jnp.where(eid[:, :, None] == experts, start_padded, 0), axis=-1) + rank
    blk_start = jnp.arange(MOE_BLOCKS, dtype=jnp.int32) * MOE_ROWS
    n_used = end_padded[-1] // MOE_ROWS
    blk_exp = jnp.sum((blk_start[:, None] >= end_padded[None, :]).astype(jnp.int32), axis=-1)
    last_exp = jnp.sum((blk_start[jnp.maximum(n_used - 1, 0)] >= end_padded).astype(jnp.int32))
    blk_exp = jnp.where(blk_start < end_padded[-1], blk_exp, last_exp).astype(jnp.int32)
    return rec, dest, end_padded, padded.astype(jnp.int32), blk_exp, n_used.reshape(1).astype(jnp.int32)


DMA_UNROLL = 8


def _tile_rows(ref, j):
    return ref.at[pl.ds(pl.multiple_of(j * ROW_TILES, ROW_TILES), ROW_TILES), :]


def _gather_rows(idx_ref, src_ref, dst_ref, sem, count):
    def body(jj, carry):
        for p in range(DMA_UNROLL):
            j = DMA_UNROLL * jj + p
            pltpu.make_async_copy(_tile_rows(src_ref, idx_ref[0, 0, j]), _tile_rows(dst_ref, j),
                                  sem).start(priority=p % 2)
        return carry
    lax.fori_loop(0, count // DMA_UNROLL, body, 0)


def _gather_wait(src_ref, dst_ref, sem, count):
    pltpu.make_async_copy(src_ref.at[pl.ds(0, count * ROW_TILES), :], dst_ref, sem).wait()


DISPATCH_TILE = 384


def _dispatch_kernel(endp_ref, padded_ref, nused_ref, idx_ref, h_ref, xb_hbm, stage_ref, zero_ref, sem_ref, zsem_ref):
    i = pl.program_id(0)
    n = pl.num_programs(0)
    tm = h_ref.shape[0]
    slot = i % 2
    blk = MOE_ROWS * ROW_TILES

    def zero_copy(b0):
        return pltpu.make_async_copy(zero_ref, xb_hbm.at[pl.ds(pl.multiple_of(b0 * ROW_TILES, ROW_TILES), blk), :], zsem_ref.at[0])

    @pl.when(i == 0)
    def _():
        zero_ref[...] = jnp.zeros_like(zero_ref)
        for e in range(N_EXPERTS):
            @pl.when(padded_ref[e] > 0)
            def _():
                zero_copy(endp_ref[e] - MOE_ROWS).start()

        def tail_start(b, carry):
            zero_copy(b * MOE_ROWS).start()
            return carry
        lax.fori_loop(nused_ref[0], MOE_BLOCKS, tail_start, 0)
        for e in range(N_EXPERTS):
            @pl.when(padded_ref[e] > 0)
            def _():
                zero_copy(endp_ref[e] - MOE_ROWS).wait()

        def tail_wait(b, carry):
            zero_copy(b * MOE_ROWS).wait()
            return carry
        lax.fori_loop(nused_ref[0], MOE_BLOCKS, tail_wait, 0)

    def wait_slot(s):
        for _ in range(TOP_K):
            pltpu.make_async_copy(stage_ref.at[s], xb_hbm.at[pl.ds(0, tm * ROW_TILES), :], sem_ref.at[s]).wait()

    @pl.when(i >= 2)
    def _():
        wait_slot(slot)

    stage = stage_ref.at[slot]
    _store_token_tiled(stage, h_ref[...], tm)

    def body(tt, carry):
        for u in range(DMA_UNROLL // TOP_K):
            t = (DMA_UNROLL // TOP_K) * tt + u
            for k in range(TOP_K):
                pltpu.make_async_copy(_tile_rows(stage, t), _tile_rows(xb_hbm, idx_ref[0, 0, k * tm + t]),
                                      sem_ref.at[slot]).start(priority=k)
        return carry
    lax.fori_loop(0, tm // (DMA_UNROLL // TOP_K), body, 0)

    @pl.when(i == n - 1)
    def _():
        wait_slot(slot)

        @pl.when(n > 1)
        def _():
            wait_slot(1 - slot)


def _tile_major_idx(dest, n_tiles, tm):
    idx = dest.reshape(n_tiles, tm, TOP_K)
    return jnp.transpose(idx, (0, 2, 1)).reshape(n_tiles, 1, TOP_K * tm)


def _dispatch(end_padded, padded, n_used, dest, h1):
    tm = DISPATCH_TILE
    n_tiles = N_ROWS // tm
    idx = _tile_major_idx(dest, n_tiles, tm)
    grid_spec = pltpu.PrefetchScalarGridSpec(
        num_scalar_prefetch=3,
        grid=(n_tiles,),
        in_specs=[pl.BlockSpec((1, 1, TOP_K * tm), lambda i, *_: (i, 0, 0), memory_space=pltpu.SMEM),
                  pl.BlockSpec((tm, D_MODEL), lambda i, *_: (i, 0))],
        out_specs=pl.BlockSpec(memory_space=pl.ANY),
        scratch_shapes=[pltpu.VMEM((2, tm * ROW_TILES, LANES), F32),
                        pltpu.VMEM((MOE_ROWS * ROW_TILES, LANES), F32),
                        pltpu.SemaphoreType.DMA((2,)),
                        pltpu.SemaphoreType.DMA((1,))])
    return pl.pallas_call(
        _dispatch_kernel,
        grid_spec=grid_spec,
        out_shape=jax.ShapeDtypeStruct((MOE_CAP * ROW_TILES, LANES), F32),
        compiler_params=_params(("arbitrary",)),
        name="dispatch",
    )(end_padded, padded, n_used, idx, h1)


def _expert_kernel(bexp_ref, nused_ref, x_ref, wg_ref, wu_ref, wd_ref, y_ref, wg16_ref, wu16_ref, wd16_ref):
    i = pl.program_id(0)
    n_used = nused_ref[0]
    rows = MOE_ROWS

    @pl.when((i == 0) | (bexp_ref[i] != bexp_ref[jnp.maximum(i - 1, 0)]))
    def _():
        wg16_ref[...] = wg_ref[...].astype(BF16)
        wu16_ref[...] = wu_ref[...].astype(BF16)
        wd16_ref[...] = wd_ref[...].astype(BF16)

    @pl.when(i < n_used)
    def _():
        x = _load_token_tiled(x_ref, 0, rows).astype(BF16)
        hid = _silu(_dot(x, wg16_ref[...])) * _dot(x, wu16_ref[...])
        _store_token_tiled(y_ref, _dot(hid.astype(BF16), wd16_ref[...]), rows)

    @pl.when(i >= n_used)
    def _():
        y_ref[...] = jnp.zeros_like(y_ref)


def _experts(layer, blk_exp, n_used, xb, w_gate, w_up, w_down):
    rows = MOE_ROWS
    blk = (rows * ROW_TILES, LANES)
    grid_spec = pltpu.PrefetchScalarGridSpec(
        num_scalar_prefetch=2,
        grid=(MOE_BLOCKS,),
        in_specs=[pl.BlockSpec(blk, lambda i, be, nu: (jnp.minimum(i, nu[0] - 1), 0)),
                  pl.BlockSpec((None, None, D_MODEL, EXPERT_FF), lambda i, be, nu: (layer, be[i], 0, 0)),
                  pl.BlockSpec((None, None, D_MODEL, EXPERT_FF), lambda i, be, nu: (layer, be[i], 0, 0)),
                  pl.BlockSpec((None, None, EXPERT_FF, D_MODEL), lambda i, be, nu: (layer, be[i], 0, 0))],
        out_specs=pl.BlockSpec(blk, lambda i, be, nu: (i, 0)),
        scratch_shapes=[pltpu.VMEM((D_MODEL, EXPERT_FF), BF16),
                        pltpu.VMEM((D_MODEL, EXPERT_FF), BF16),
                        pltpu.VMEM((EXPERT_FF, D_MODEL), BF16)])
    return pl.pallas_call(
        _expert_kernel,
        grid_spec=grid_spec,
        out_shape=jax.ShapeDtypeStruct((MOE_CAP * ROW_TILES, LANES), F32),
        compiler_params=_params(("arbitrary",)),
        name="experts",
    )(blk_exp, n_used, xb, w_gate, w_up, w_down)


def _combine_kernel(idx_ref, idxn_ref, h1_ref, rec_ref, y_hbm, lng_ref, lnb_ref, o_ref, ybuf_ref, sem_ref, *, n_tiles):
    i = pl.program_id(0)
    tm = h1_ref.shape[0]
    slot = i % 2
    count = TOP_K * tm

    @pl.when(i == 0)
    def _():
        _gather_rows(idx_ref, y_hbm, ybuf_ref.at[0], sem_ref.at[0], count)

    @pl.when(i + 1 < n_tiles)
    def _():
        _gather_rows(idxn_ref, y_hbm, ybuf_ref.at[1 - slot], sem_ref.at[1 - slot], count)

    _gather_wait(y_hbm, ybuf_ref.at[slot], sem_ref.at[slot], count)
    y = DN_ALPHA * h1_ref[...]
    for k in range(TOP_K):
        wk = rec_ref[:, REC_WEIGHT + k:REC_WEIGHT + k + 1]
        y = y + wk * _load_token_tiled(ybuf_ref.at[slot], k * tm * ROW_TILES, tm)
    o_ref[...] = _layer_norm(y, lng_ref[...], lnb_ref[...])


def _combine(dest, rec, h1, yb, lng, lnb, tm, row0, n):
    n_tiles = n // tm
    blk0 = row0 // tm
    idx = _tile_major_idx(dest[row0:row0 + n], n_tiles, tm)
    last = n_tiles - 1
    return pl.pallas_call(
        functools.partial(_combine_kernel, n_tiles=n_tiles),
        grid=(n_tiles,),
        in_specs=[pl.BlockSpec((1, 1, TOP_K * tm), lambda i: (i, 0, 0), memory_space=pltpu.SMEM),
                  pl.BlockSpec((1, 1, TOP_K * tm), lambda i: (jnp.minimum(i + 1, last), 0, 0),
                               memory_space=pltpu.SMEM),
                  pl.BlockSpec((tm, D_MODEL), lambda i: (blk0 + i, 0)),
                  pl.BlockSpec((tm, LANES), lambda i: (blk0 + i, 0)),
                  pl.BlockSpec(memory_space=pl.ANY),
                  _const_spec((1, D_MODEL)), _const_spec((1, D_MODEL))],
        out_specs=pl.BlockSpec((tm, D_MODEL), lambda i: (i, 0)),
        out_shape=jax.ShapeDtypeStruct((n, D_MODEL), F32),
        scratch_shapes=[pltpu.VMEM((2, TOP_K * tm * ROW_TILES, LANES), F32),
                        pltpu.SemaphoreType.DMA((2,))],
        compiler_params=_params(("arbitrary",)),
        name="combine",
    )(idx, idx, h1, rec, yb, lng, lnb)


PROMPT_TILE = 512
COMBINE_TILE = 256


def _row(v, width=None):
    v = v.reshape(1, -1).astype(F32)
    if width is not None and v.shape[1] < width:
        v = jnp.pad(v, ((0, 0), (0, width - v.shape[1])))
    return v


def _layer(layer, hp, hs, ck, cv, conv_st, rec_all, pool_st, lw, w_packed, experts_w):
    alog_row = _row(lw['gdn_a_log'], LANES)
    dtb_row = _row(lw['gdn_dt_bias'], LANES)
    norm_g = _row(lw['gdn_norm_g'])
    pool_scale = _row(lw['pool_scale'])
    sinks = lw['attn_sinks'].astype(F32)
    kvw = A_KV_HEADS * A_HEAD_DIM

    gates_p, bqkv_p, aq_p, z_p, cp_p, kv_p, ab_p = _in_proj(layer, hp, w_packed, PROMPT_TILE)
    oa_p = _attn_prompt(sinks, aq_p, kv_p)
    ob_p, rec_p = _gdn_prompt(bqkv_p, z_p, ab_p, lw['gdn_conv_w'], alog_row, dtb_row, norm_g)
    oc_p = _pool_prompt(cp_p, lw['pool_w'], pool_scale)
    kv_tail = kv_p.reshape(BATCH, SEQ, 2, A_KV_HEADS, A_HEAD_DIM)[:, SEQ - WINDOW:]
    new_k_p, new_v_p = kv_tail[:, :, 0], kv_tail[:, :, 1]
    new_conv_p = bqkv_p.reshape(BATCH, SEQ, 3 * B_WIDTH)[:, SEQ - (CONV_W - 1):]
    new_pool_p = cp_p.reshape(BATCH, SEQ, C_WIDTH)[:, SEQ - POOL_HIST:]

    gates_s, bqkv_s, aq_s, z_s, cp_s, kv_s, ab_s = _in_proj(layer, hs, w_packed, DEC_BATCH)
    oa_s3, nk_s, nv_s = _attn_sample(layer, sinks, aq_s.astype(F32).reshape(DEC_BATCH, A_HEADS, A_HEAD_DIM),
                                     kv_s[:, :kvw], kv_s[:, kvw:], ck, cv)
    oa_s = oa_s3.reshape(DEC_BATCH, A_WIDTH).astype(BF16)
    ob_s, nconv_s, rec_s = _gdn_sample(layer, jnp.transpose(conv_st, (1, 0, 2)), bqkv_s, z_s, ab_s, lw['gdn_conv_w'],
                                       alog_row, dtb_row, norm_g, rec_all)
    oc_s, npool_s = _pool_sample(jnp.transpose(pool_st, (1, 0, 2)), cp_s, lw['pool_w'], pool_scale)
    new_k_s = nk_s.reshape(DEC_BATCH, WINDOW, A_KV_HEADS, A_HEAD_DIM)
    new_v_s = nv_s.reshape(DEC_BATCH, WINDOW, A_KV_HEADS, A_HEAD_DIM)
    new_conv_s = jnp.transpose(nconv_s, (1, 0, 2))
    new_pool_s = jnp.transpose(npool_s, (1, 0, 2))

    wa, wb, wc = (lw[k].astype(BF16) for k in ('w_branch_a', 'w_branch_b', 'w_branch_c'))
    wo = lw['w_o'].astype(BF16)
    wr = jnp.concatenate([lw['router_group_w'],
                          jnp.transpose(lw['router_expert_w'], (1, 0, 2)).reshape(D_MODEL, N_EXPERTS),
                          jnp.zeros((D_MODEL, LANES - ROUTER_COLS), F32)], axis=-1)
    br = _row(jnp.concatenate([lw['router_group_b'], lw['router_expert_b'].reshape(-1)]), LANES)
    ln1g, ln1b, ln2g, ln2b = (_row(lw[k]) for k in ('ln1_g', 'ln1_b', 'ln2_g', 'ln2_b'))
    h1, logits = _merge((hp, oa_p, ob_p, oc_p, gates_p), (hs, oa_s, ob_s, oc_s, gates_s),
                        wa, wb, wc, wo, ln1g, ln1b, wr, br, PROMPT_TILE)
    rec, dest, end_padded, padded, blk_exp, n_used = _route(logits)
    xb = _dispatch(end_padded, padded, n_used, dest, h1)
    yb = _experts(layer, blk_exp, n_used, xb, *experts_w)
    h2_p = _combine(dest, rec, h1, yb, ln2g, ln2b, COMBINE_TILE, 0, N_PROMPT)
    h2_s = _combine(dest, rec, h1, yb, ln2g, ln2b, DEC_BATCH, N_PROMPT, DEC_BATCH)
    return (h2_p, h2_s, (new_k_p, new_v_p, new_conv_p, rec_p, new_pool_p),
            (new_k_s, new_v_s, new_conv_s, rec_s, new_pool_s))


def kernel(x_prompt, x_sample, cache_attn_k, cache_attn_v, state_gdn_conv, state_gdn_rec, state_pool, w_in, attn_sinks, gdn_conv_w, gdn_a_log, gdn_dt_bias, gdn_norm_g, pool_w, pool_scale, w_branch_a, w_branch_b, w_branch_c, w_o, ln1_g, ln1_b, router_group_w, router_group_b, router_expert_w, router_expert_b, w_gate, w_up, w_down, ln2_g, ln2_b):
    weights = dict(attn_sinks=attn_sinks, gdn_conv_w=gdn_conv_w, gdn_a_log=gdn_a_log,
                   gdn_dt_bias=gdn_dt_bias, gdn_norm_g=gdn_norm_g, pool_w=pool_w, pool_scale=pool_scale,
                   w_branch_a=w_branch_a, w_branch_b=w_branch_b, w_branch_c=w_branch_c, w_o=w_o,
                   ln1_g=ln1_g, ln1_b=ln1_b, router_group_w=router_group_w, router_group_b=router_group_b,
                   router_expert_w=router_expert_w, router_expert_b=router_expert_b,
                   ln2_g=ln2_g, ln2_b=ln2_b)
    hp = x_prompt.reshape(N_PROMPT, D_MODEL)
    hs = x_sample.reshape(DEC_BATCH, D_MODEL)
    kvw = A_KV_HEADS * A_HEAD_DIM
    ck = cache_attn_k.reshape(DEPTH, DEC_BATCH, WINDOW, kvw)
    cv = cache_attn_v.reshape(DEPTH, DEC_BATCH, WINDOW, kvw)
    w_packed = _pack_w_in(w_in)
    st_p = [[] for _ in range(5)]
    st_s = [[] for _ in range(5)]
    for l in range(DEPTH):
        lw = {k: v[l] for k, v in weights.items()}
        hp, hs, sp, ss = _layer(l, hp, hs, ck, cv, state_gdn_conv[l], state_gdn_rec, state_pool[l], lw,
                                w_packed, (w_gate, w_up, w_down))
        for j in range(5):
            st_p[j].append(sp[j])
            st_s[j].append(ss[j])
    return (hp.reshape(BATCH, SEQ, D_MODEL), hs.reshape(DEC_BATCH, 1, D_MODEL),
            *(jnp.stack(t) for t in st_p), *(jnp.stack(t) for t in st_s))
```

```python
import functools

import jax
import jax.numpy as jnp
import numpy as np
from jax import lax
from jax.experimental import pallas as pl
from jax.experimental.pallas import tpu as pltpu

F32 = jnp.float32
BF16 = jnp.bfloat16

D_MODEL = 1024
BATCH = 2
SEQ = 8192
DEPTH = 2
DEC_BATCH = 128
PAST_LEN = 16384
WINDOW = 128
A_HEADS = 8
A_KV_HEADS = 2
A_HEAD_DIM = 64
A_GROUP = A_HEADS // A_KV_HEADS
A_WIDTH = A_HEADS * A_HEAD_DIM
B_HEADS = 4
B_HEAD_DIM = 128
B_WIDTH = B_HEADS * B_HEAD_DIM
CONV_W = 4
GDN_CHUNK = 64
POOL_WINDOWS = (2, 4, 8, 16)
POOL_GROUPS = 4
C_WIDTH = 512
POOL_GROUP_DIM = C_WIDTH // POOL_GROUPS
POOL_HIST = max(POOL_WINDOWS) - 1
N_BRANCHES = 3
IN_SPLIT_SIZES = (A_WIDTH, A_KV_HEADS * A_HEAD_DIM, A_KV_HEADS * A_HEAD_DIM, 3 * B_WIDTH,
                  B_WIDTH, B_HEADS, B_HEADS, C_WIDTH, N_BRANCHES * D_MODEL)
N_GROUPS = 4
EXPERTS_PER_GROUP = 8
N_EXPERTS = N_GROUPS * EXPERTS_PER_GROUP
TOP_K = 2
EXPERT_FF = 512
DN_ALPHA = (2.0 * DEPTH) ** 0.25
LN_EPS = 1e-5
RMS_EPS = 1e-6
NEG_INF = -1e30

LANES = 128
SUBLANES = 8
ROW_TILES = D_MODEL // LANES
VMEM_LIMIT = 56 * 1024 * 1024

N_PROMPT = BATCH * SEQ
N_ROWS = N_PROMPT + DEC_BATCH

SEG_GATES = (0, N_BRANCHES * D_MODEL)
SEG_BQKV = (3072, 3 * B_WIDTH)
SEG_AQ = (4608, A_WIDTH)
SEG_Z = (5120, B_WIDTH)
SEG_CP = (5632, C_WIDTH)
SEG_KV = (6144, 2 * A_KV_HEADS * A_HEAD_DIM)
SEG_AB = (6400, LANES)
IN_PACKED = 6528

MOE_ROWS = 256
MOE_ASSIGN = N_ROWS * TOP_K
MOE_CAP = -(-MOE_ASSIGN // MOE_ROWS) * MOE_ROWS + N_EXPERTS * MOE_ROWS
MOE_BLOCKS = MOE_CAP // MOE_ROWS
ROUTER_COLS = N_GROUPS + N_EXPERTS


def _sigmoid(x):
    return 1.0 / (1.0 + jnp.exp(-x))


def _silu(x):
    return x * _sigmoid(x)


def _dot(a, b):
    return jnp.dot(a, b, preferred_element_type=F32)


def _dot_nt(a, b):
    return lax.dot_general(a, b, (((1,), (1,)), ((), ())), preferred_element_type=F32)


def _dot_tn(a, b):
    return lax.dot_general(a, b, (((0,), (0,)), ((), ())), preferred_element_type=F32)


def _split3(x):
    hi = x.astype(BF16)
    r = x - hi.astype(F32)
    mid = r.astype(BF16)
    lo = (r - mid.astype(F32)).astype(BF16)
    return hi, mid, lo


def _params(sem=None):
    return pltpu.CompilerParams(dimension_semantics=sem, vmem_limit_bytes=VMEM_LIMIT)


def _const_spec(shape):
    nd = len(shape)
    return pl.BlockSpec(shape, lambda *_: (0,) * nd)


def _pack_w_in(w):
    w = w.astype(BF16)
    aq, ak, av, bqkv, bz, ba, bb, cp, gate = jnp.split(w, np.cumsum(IN_SPLIT_SIZES)[:-1].tolist(), axis=-1)
    aq = aq * (A_HEAD_DIM ** -0.5)
    ab = jnp.concatenate([ba, bb, jnp.zeros(w.shape[:-1] + (LANES - 2 * B_HEADS,), w.dtype)], -1)
    return jnp.concatenate([gate, bqkv, aq, bz, cp, ak, av, ab], -1)


def _in_proj_kernel(x_ref, w_ref, gates_ref, bqkv_ref, aq_ref, z_ref, cp_ref, kv_ref, ab_ref):
    x = x_ref[...].astype(BF16)

    def mm(lo, n):
        return _dot(x, w_ref[:, lo:lo + n])

    step = 512
    for c in range(SEG_GATES[1] // step):
        gates_ref[:, c * step:(c + 1) * step] = _sigmoid(mm(SEG_GATES[0] + c * step, step)).astype(BF16)
    for c in range(SEG_BQKV[1] // step):
        bqkv_ref[:, c * step:(c + 1) * step] = mm(SEG_BQKV[0] + c * step, step)
    aq_ref[...] = mm(*SEG_AQ).astype(BF16)
    z_ref[...] = mm(*SEG_Z)
    cp_ref[...] = mm(*SEG_CP)
    kv_ref[...] = mm(*SEG_KV)
    ab_ref[...] = mm(*SEG_AB)


def _in_proj(layer, x, w_packed, tm):
    n = x.shape[0]
    widths = (SEG_GATES[1], SEG_BQKV[1], SEG_AQ[1], SEG_Z[1], SEG_CP[1], SEG_KV[1], SEG_AB[1])
    dtypes = (BF16, F32, BF16, F32, F32, F32, F32)
    return pl.pallas_call(
        _in_proj_kernel,
        grid=(n // tm,),
        in_specs=[pl.BlockSpec((tm, D_MODEL), lambda i: (i, 0)),
                  pl.BlockSpec((None, D_MODEL, IN_PACKED), lambda i: (layer, 0, 0), pipeline_mode=pl.Buffered(1))],
        out_specs=[pl.BlockSpec((tm, w), lambda i: (i, 0)) for w in widths],
        out_shape=[jax.ShapeDtypeStruct((n, w), dt) for w, dt in zip(widths, dtypes)],
        compiler_params=_params(("arbitrary",)),
        name="in_proj",
    )(x, w_packed)


def _alibi_slope(h):
    return 2.0 ** (-8.0 * (h + 1) / A_HEADS)


def _attn_prompt_kernel(sink_ref, q_ref, kvc_ref, kvp_ref, o_ref):
    n = pl.program_id(1)
    kvc = kvc_ref[...]
    kvp = kvp_ref[...]
    t = lax.broadcasted_iota(jnp.int32, (WINDOW, 2 * WINDOW), 0)
    s = lax.broadcasted_iota(jnp.int32, (WINDOW, 2 * WINDOW), 1)
    dist = WINDOW + t - s
    valid = (dist >= 0) & (dist <= WINDOW) & ((s >= WINDOW) | (n > 0))
    distf = dist.astype(F32)
    hd = A_HEAD_DIM
    vo = A_KV_HEADS * hd
    outs = []
    for kh in range(A_KV_HEADS):
        k = jnp.concatenate([kvp[:, kh * hd:(kh + 1) * hd], kvc[:, kh * hd:(kh + 1) * hd]], axis=0).astype(BF16)
        v = jnp.concatenate([kvp[:, vo + kh * hd:vo + (kh + 1) * hd],
                             kvc[:, vo + kh * hd:vo + (kh + 1) * hd]], axis=0).astype(BF16)
        for g in range(A_GROUP):
            h = kh * A_GROUP + g
            sc = _dot_nt(q_ref[:, h * hd:(h + 1) * hd], k)
            sc = jnp.where(valid, sc - _alibi_slope(h) * distf, NEG_INF)
            sink = sink_ref[h]
            m = jnp.maximum(jnp.max(sc, axis=-1, keepdims=True), sink)
            p = jnp.exp(sc - m)
            denom = jnp.sum(p, axis=-1, keepdims=True) + jnp.exp(sink - m)
            outs.append(_dot(p.astype(BF16), v) / denom)
    o_ref[...] = jnp.concatenate(outs, axis=-1).astype(BF16)


def _attn_prompt(sinks, aq, kv):
    nb = SEQ // WINDOW
    return pl.pallas_call(
        _attn_prompt_kernel,
        grid=(BATCH, nb),
        in_specs=[pl.BlockSpec(memory_space=pltpu.SMEM),
                  pl.BlockSpec((WINDOW, A_WIDTH), lambda b, n: (b * nb + n, 0)),
                  pl.BlockSpec((WINDOW, SEG_KV[1]), lambda b, n: (b * nb + n, 0)),
                  pl.BlockSpec((WINDOW, SEG_KV[1]), lambda b, n: (b * nb + jnp.maximum(n - 1, 0), 0))],
        out_specs=pl.BlockSpec((WINDOW, A_WIDTH), lambda b, n: (b * nb + n, 0)),
        out_shape=jax.ShapeDtypeStruct((N_PROMPT, A_WIDTH), BF16),
        compiler_params=_params(("arbitrary", "arbitrary")),
        name="attn_prompt",
    )(sinks, aq, kv, kv)


ATTN_S_BLOCK = 8


def _attn_sample_kernel(sink_ref, q_ref, kn_ref, vn_ref, ck_ref, cv_ref, o_ref, nk_ref, nv_ref):
    hd = A_HEAD_DIM
    row = lax.broadcasted_iota(jnp.int32, (A_HEADS, 1), 0)
    slope = jnp.zeros((A_HEADS, 1), F32)
    sink = jnp.zeros((A_HEADS, 1), F32)
    for h in range(A_HEADS):
        slope = jnp.where(row == h, _alibi_slope(h), slope)
        sink = jnp.where(row == h, sink_ref[h], sink)
    pos = lax.broadcasted_iota(jnp.int32, (A_HEADS, WINDOW), 1)
    bias = -slope * (WINDOW - pos).astype(F32)
    first = row < A_GROUP
    for b in range(ATTN_S_BLOCK):
        kc = ck_ref[b]
        vc = cv_ref[b]
        qf = q_ref[b]
        q = qf.astype(BF16)
        kn = kn_ref[b:b + 1, :]
        vn = vn_ref[b:b + 1, :]
        s0 = _dot_nt(q, kc[:, 0:hd].astype(BF16))
        s1 = _dot_nt(q, kc[:, hd:2 * hd].astype(BF16))
        sc = jnp.where(first, s0, s1) + bias
        sn0 = jnp.sum(qf * kn[:, 0:hd], axis=-1, keepdims=True)
        sn1 = jnp.sum(qf * kn[:, hd:2 * hd], axis=-1, keepdims=True)
        sn = jnp.where(first, sn0, sn1)
        m = jnp.maximum(jnp.maximum(jnp.max(sc, axis=-1, keepdims=True), sn), sink)
        p = jnp.exp(sc - m)
        pn = jnp.exp(sn - m)
        denom = jnp.sum(p, axis=-1, keepdims=True) + pn + jnp.exp(sink - m)
        pb = p.astype(BF16)
        o0 = _dot(pb, vc[:, 0:hd].astype(BF16)) + pn * vn[:, 0:hd]
        o1 = _dot(pb, vc[:, hd:2 * hd].astype(BF16)) + pn * vn[:, hd:2 * hd]
        o_ref[b] = jnp.where(first, o0, o1) / denom
        nk_ref[b, 0:WINDOW - 1, :] = kc[1:WINDOW, :]
        nk_ref[b, WINDOW - 1:WINDOW, :] = kn
        nv_ref[b, 0:WINDOW - 1, :] = vc[1:WINDOW, :]
        nv_ref[b, WINDOW - 1:WINDOW, :] = vn


def _attn_sample(layer, sinks, q3, kn, vn, ck, cv):
    bs = ATTN_S_BLOCK
    kvw = A_KV_HEADS * A_HEAD_DIM
    return pl.pallas_call(
        _attn_sample_kernel,
        grid=(DEC_BATCH // bs,),
        in_specs=[pl.BlockSpec(memory_space=pltpu.SMEM),
                  pl.BlockSpec((bs, A_HEADS, A_HEAD_DIM), lambda i: (i, 0, 0)),
                  pl.BlockSpec((bs, kvw), lambda i: (i, 0)),
                  pl.BlockSpec((bs, kvw), lambda i: (i, 0)),
                  pl.BlockSpec((None, bs, WINDOW, kvw), lambda i: (layer, i, 0, 0)),
                  pl.BlockSpec((None, bs, WINDOW, kvw), lambda i: (layer, i, 0, 0))],
        out_specs=[pl.BlockSpec((bs, A_HEADS, A_HEAD_DIM), lambda i: (i, 0, 0)),
                   pl.BlockSpec((bs, WINDOW, kvw), lambda i: (i, 0, 0)),
                   pl.BlockSpec((bs, WINDOW, kvw), lambda i: (i, 0, 0))],
        out_shape=[jax.ShapeDtypeStruct((DEC_BATCH, A_HEADS, A_HEAD_DIM), F32),
                   jax.ShapeDtypeStruct((DEC_BATCH, WINDOW, kvw), F32),
                   jax.ShapeDtypeStruct((DEC_BATCH, WINDOW, kvw), F32)],
        compiler_params=_params(("arbitrary",)),
        name="attn_sample",
    )(sinks, q3, kn, vn, ck, cv)


GDN_TILE = 256
GDN_STACK = B_HEADS * GDN_CHUNK


def _gdn_gates(ab, alog, dtb):
    xs = ab + dtb
    softplus = jnp.maximum(xs, 0.0) + jnp.log(1.0 + jnp.exp(-jnp.abs(xs)))
    g = -jnp.exp(alog) * softplus
    beta = _sigmoid(ab)
    return g, beta


def _l2norm(x):
    return x * lax.rsqrt(jnp.sum(x * x, axis=-1, keepdims=True) + RMS_EPS)


def _gdn_out_norm(o, z, norm_g):
    return o * lax.rsqrt(jnp.mean(o * o, axis=-1, keepdims=True) + RMS_EPS) * norm_g * _silu(z)


def _gdn_prompt_kernel(x_ref, z_ref, ab_ref, cw_ref, alog_ref, dtb_ref, ng_ref,
                       o_ref, sfin_ref, xbuf_ref, s_ref):
    n = pl.program_id(1)
    t_rows = GDN_TILE
    c_rows = GDN_CHUNK
    hd = B_HEAD_DIM
    pad = SUBLANES

    @pl.when(n == 0)
    def _():
        xbuf_ref[0:pad, :] = jnp.zeros((pad, 3 * B_WIDTH), F32)
        s_ref[...] = jnp.zeros_like(s_ref)

    x = x_ref[...]
    xbuf_ref[pad:pad + t_rows, :] = x
    cw = cw_ref[...]
    conv = x * cw[CONV_W - 1:CONV_W, :]
    for j in range(CONV_W - 1):
        back = CONV_W - 1 - j
        conv = conv + xbuf_ref[pad - back:pad - back + t_rows, :] * cw[j:j + 1, :]
    xbuf_ref[0:pad, :] = x[t_rows - pad:t_rows, :]
    conv = _silu(conv)

    g_all, beta_all = _gdn_gates(ab_ref[...], alog_ref[...], dtb_ref[...])
    ri = lax.broadcasted_iota(jnp.int32, (t_rows, t_rows), 0)
    ci = lax.broadcasted_iota(jnp.int32, (t_rows, t_rows), 1)
    tri = jnp.where((ri // c_rows == ci // c_rows) & (ri >= ci), 1.0, 0.0).astype(BF16)
    g_hi, g_mid, g_lo = _split3(g_all)
    gcum_all = _dot(tri, g_hi) + _dot(tri, g_mid) + _dot(tri, g_lo)

    qn, kn, vv = [], [], []
    for h in range(B_HEADS):
        qn.append(_l2norm(conv[:, h * hd:(h + 1) * hd]) * (hd ** -0.5))
        kn.append(_l2norm(conv[:, B_WIDTH + h * hd:B_WIDTH + (h + 1) * hd]))
        vv.append(conv[:, 2 * B_WIDTH + h * hd:2 * B_WIDTH + (h + 1) * hd])

    st = GDN_STACK
    nc = t_rows // c_rows
    r = lax.broadcasted_iota(jnp.int32, (st, st), 0)
    c = lax.broadcasted_iota(jnp.int32, (st, st), 1)
    same = (r // c_rows) == (c // c_rows)
    incl = same & (r >= c)
    strict = same & (r > c)

    def chunk_rows(a, ck):
        return a[ck * c_rows:(ck + 1) * c_rows]

    def stack(parts):
        return jnp.stack([jnp.concatenate([chunk_rows(p, ck) for p in parts], axis=0) for ck in range(nc)], axis=0)

    def stack_col(a, lane0):
        return stack([a[:, lane0 + h:lane0 + h + 1] for h in range(B_HEADS)])

    qs, ks, vs = stack(qn), stack(kn), stack(vv)
    gcol = stack_col(gcum_all, 0)
    bcol = stack_col(beta_all, B_HEADS)
    glast = [[gcum_all[(ck + 1) * c_rows - 1:(ck + 1) * c_rows, h:h + 1] for h in range(B_HEADS)] for ck in range(nc)]
    glast_col = jnp.stack([jnp.concatenate([jnp.broadcast_to(gl, (c_rows, 1)) for gl in glast[ck]], axis=0)
                           for ck in range(nc)], axis=0)
    gt_hi, gt_mid, gt_lo = _split3(g_all.T)
    tri_t = jnp.where((ri // c_rows == ci // c_rows) & (ri <= ci), 1.0, 0.0).astype(BF16)
    gcum_t = _dot(gt_hi, tri_t) + _dot(gt_mid, tri_t) + _dot(gt_lo, tri_t)
    grow = jnp.stack([jnp.concatenate([gcum_t[h:h + 1, ck * c_rows:(ck + 1) * c_rows] for h in range(B_HEADS)], axis=1)
                      for ck in range(nc)], axis=0)

    def bmm(a, b):
        return jnp.einsum('cij,cjk->cik', a, b, preferred_element_type=F32)

    def bmm_nt(a, b):
        return jnp.einsum('cik,cjk->cij', a, b, preferred_element_type=F32)

    decay = jnp.where(incl, jnp.exp(jnp.where(incl, gcol - grow, 0.0)), 0.0)
    exp_g = jnp.exp(gcol)
    kb = ks * bcol
    vb = vs * bcol
    ks16 = ks.astype(BF16)
    kk = bmm_nt(kb.astype(BF16), ks16)
    qk = bmm_nt(qs.astype(BF16), ks16)
    m_mat = jnp.where(strict, kk * decay, 0.0)
    m16 = m_mat.astype(BF16)
    zero16 = jnp.zeros((), BF16)
    p16 = jnp.where(r == c, 1.0, 0.0).astype(BF16) - jnp.where((r // 2) == (c // 2), m16, zero16)
    blk = 2
    while blk < c_rows:
        lvl = ((r // (2 * blk)) == (c // (2 * blk))) & ((r // blk) != (c // blk))
        a_off = jnp.where(lvl, m16, zero16)
        p16 = p16 - bmm(bmm(p16, a_off).astype(BF16), p16).astype(BF16)
        blk *= 2
    uw = bmm(p16, jnp.concatenate([vb, kb * exp_g], axis=2).astype(BF16))
    aqk = (qk * decay).astype(BF16)
    qg = qs * exp_g
    kd = (ks * jnp.exp(glast_col - gcol)).astype(BF16)

    o_chunks = []
    for ck in range(nc):
        vnew, qs_s = [], []
        for h in range(B_HEADS):
            hs = slice(h * c_rows, (h + 1) * c_rows)
            wq = jnp.concatenate([uw[ck, hs, hd:2 * hd], qg[ck, hs]], axis=0).astype(BF16)
            rs = _dot(wq, s_ref[h].astype(BF16))
            vnew.append(uw[ck, hs, 0:hd] - rs[0:c_rows])
            qs_s.append(rs[c_rows:2 * c_rows])
        vnew16 = jnp.concatenate(vnew, axis=0).astype(BF16)
        o_st = jnp.concatenate(qs_s, axis=0) + _dot(aqk[ck], vnew16)
        for h in range(B_HEADS):
            hs = slice(h * c_rows, (h + 1) * c_rows)
            s_ref[h] = s_ref[h] * jnp.exp(glast[ck][h]) + _dot_tn(kd[ck, hs], vnew16[hs])
        o_chunks.append([o_st[h * c_rows:(h + 1) * c_rows] for h in range(B_HEADS)])

    ng = ng_ref[...]
    z = z_ref[...]
    outs = []
    for h in range(B_HEADS):
        o_h = jnp.concatenate([oc[h] for oc in o_chunks], axis=0)
        outs.append(_gdn_out_norm(o_h, z[:, h * hd:(h + 1) * hd], ng))
    o_ref[...] = jnp.concatenate(outs, axis=-1).astype(BF16)
    sfin_ref[0] = s_ref[...]


def _gdn_prompt(bqkv, z, ab, conv_w, alog_row, dtb_row, norm_g):
    nt = SEQ // GDN_TILE
    t = GDN_TILE
    return pl.pallas_call(
        _gdn_prompt_kernel,
        grid=(BATCH, nt),
        in_specs=[pl.BlockSpec((t, 3 * B_WIDTH), lambda b, n: (b * nt + n, 0)),
                  pl.BlockSpec((t, B_WIDTH), lambda b, n: (b * nt + n, 0)),
                  pl.BlockSpec((t, LANES), lambda b, n: (b * nt + n, 0)),
                  _const_spec((CONV_W, 3 * B_WIDTH)),
                  _const_spec((1, LANES)),
                  _const_spec((1, LANES)),
                  _const_spec((1, B_HEAD_DIM))],
        out_specs=[pl.BlockSpec((t, B_WIDTH), lambda b, n: (b * nt + n, 0)),
                   pl.BlockSpec((1, B_HEADS, B_HEAD_DIM, B_HEAD_DIM), lambda b, n: (b, 0, 0, 0))],
        out_shape=[jax.ShapeDtypeStruct((N_PROMPT, B_WIDTH), BF16),
                   jax.ShapeDtypeStruct((BATCH, B_HEADS, B_HEAD_DIM, B_HEAD_DIM), F32)],
        scratch_shapes=[pltpu.VMEM((SUBLANES + t, 3 * B_WIDTH), F32),
                        pltpu.VMEM((B_HEADS, B_HEAD_DIM, B_HEAD_DIM), F32)],
        compiler_params=_params(("arbitrary", "arbitrary")),
        name="gdn_prompt",
    )(bqkv, z, ab, conv_w, alog_row, dtb_row, norm_g)


GDN_S_BLOCK = 8


def _gdn_sample_kernel(st_ref, x_ref, z_ref, ab_ref, cw_ref, alog_ref, dtb_ref, ng_ref, s_ref,
                       o_ref, nst_ref, ns_ref):
    hd = B_HEAD_DIM
    x = x_ref[...]
    cw = cw_ref[...]
    conv = x * cw[CONV_W - 1:CONV_W, :]
    for j in range(CONV_W - 1):
        conv = conv + st_ref[j] * cw[j:j + 1, :]
    for j in range(CONV_W - 2):
        nst_ref[j] = st_ref[j + 1]
    nst_ref[CONV_W - 2] = x
    conv = _silu(conv)
    g_all, beta_all = _gdn_gates(ab_ref[...], alog_ref[...], dtb_ref[...])
    eg_all = jnp.exp(g_all)
    r = lax.broadcasted_iota(jnp.int32, (hd, hd), 0)
    c = lax.broadcasted_iota(jnp.int32, (hd, hd), 1)
    eye = r == c
    zpad = jnp.zeros((SUBLANES - 2, hd), F32)
    ng = ng_ref[...]
    z = z_ref[...]
    outs = []
    for h in range(B_HEADS):
        q = _l2norm(conv[:, h * hd:(h + 1) * hd]) * (hd ** -0.5)
        k = _l2norm(conv[:, B_WIDTH + h * hd:B_WIDTH + (h + 1) * hd])
        v = conv[:, 2 * B_WIDTH + h * hd:2 * B_WIDTH + (h + 1) * hd]
        beta = beta_all[:, B_HEADS + h:B_HEADS + h + 1]
        eg = eg_all[:, h:h + 1]
        qk = jnp.sum(q * k, axis=-1, keepdims=True)
        rows = []
        for b in range(GDN_S_BLOCK):
            kb = k[b:b + 1]
            s_bh = s_ref[b, h]
            kq = jnp.concatenate([kb, q[b:b + 1], zpad], axis=0).astype(BF16)
            rs = _dot(kq, s_bh.astype(BF16))
            bb = beta[b:b + 1]
            egb = eg[b:b + 1]
            v_new = bb * v[b:b + 1] - (bb * egb) * rs[0:1]
            rows.append(egb * rs[1:2] + qk[b:b + 1] * v_new)
            kdiag = jnp.where(eye, jnp.broadcast_to(kb, (hd, hd)), 0.0).astype(BF16)
            outer = _dot(kdiag, jnp.broadcast_to(v_new, (hd, hd)).astype(BF16))
            ns_ref[b, h] = s_bh * egb + outer
        o_h = jnp.concatenate(rows, axis=0)
        outs.append(_gdn_out_norm(o_h, z[:, h * hd:(h + 1) * hd], ng))
    o_ref[...] = jnp.concatenate(outs, axis=-1).astype(BF16)


def _gdn_sample(layer, conv_st, bqkv, z, ab, conv_w, alog_row, dtb_row, norm_g, s0):
    bs = GDN_S_BLOCK
    cw3 = 3 * B_WIDTH
    return pl.pallas_call(
        _gdn_sample_kernel,
        grid=(DEC_BATCH // bs,),
        in_specs=[pl.BlockSpec((CONV_W - 1, bs, cw3), lambda i: (0, i, 0)),
                  pl.BlockSpec((bs, cw3), lambda i: (i, 0)),
                  pl.BlockSpec((bs, B_WIDTH), lambda i: (i, 0)),
                  pl.BlockSpec((bs, LANES), lambda i: (i, 0)),
                  _const_spec((CONV_W, cw3)),
                  _const_spec((1, LANES)),
                  _const_spec((1, LANES)),
                  _const_spec((1, B_HEAD_DIM)),
                  pl.BlockSpec((None, bs, B_HEADS, B_HEAD_DIM, B_HEAD_DIM), lambda i: (layer, i, 0, 0, 0))],
        out_specs=[pl.BlockSpec((bs, B_WIDTH), lambda i: (i, 0)),
                   pl.BlockSpec((CONV_W - 1, bs, cw3), lambda i: (0, i, 0)),
                   pl.BlockSpec((bs, B_HEADS, B_HEAD_DIM, B_HEAD_DIM), lambda i: (i, 0, 0, 0))],
        out_shape=[jax.ShapeDtypeStruct((DEC_BATCH, B_WIDTH), BF16),
                   jax.ShapeDtypeStruct((CONV_W - 1, DEC_BATCH, cw3), F32),
                   jax.ShapeDtypeStruct((DEC_BATCH, B_HEADS, B_HEAD_DIM, B_HEAD_DIM), F32)],
        compiler_params=_params(("arbitrary",)),
        name="gdn_sample",
    )(conv_st, bqkv, z, ab, conv_w, alog_row, dtb_row, norm_g, s0)


POOL_TILE = 512
POOL_PAD = 16


def _pool_project(d, pw_ref, scale):
    outs = []
    for gi in range(POOL_GROUPS):
        lo = gi * POOL_GROUP_DIM
        outs.append(_dot(d[:, lo:lo + POOL_GROUP_DIM].astype(BF16), pw_ref[gi].astype(BF16)))
    return jnp.concatenate(outs, axis=-1) * scale


def _pool_prompt_kernel(x_ref, pw_ref, ps_ref, o_ref, buf_ref):
    n = pl.program_id(1)
    t = POOL_TILE
    pad = POOL_PAD
    gd = POOL_GROUP_DIM

    @pl.when(n == 0)
    def _():
        buf_ref[0:pad, :] = jnp.zeros((pad, C_WIDTH), F32)

    x = x_ref[...]
    buf_ref[pad:pad + t, :] = x
    s1 = buf_ref[1:pad + t, :]
    s2 = s1 + buf_ref[0:pad + t - 1, :]
    s4 = s2[2:, gd:] + s2[:-2, gd:]
    s8 = s4[4:, gd:] + s4[:-4, gd:]
    s16 = s8[8:, gd:] + s8[:-8, gd:]
    sums = (s2[15:, 0:gd], s4[13:, 0:gd], s8[9:, 0:gd], s16[1:, :])
    pos = n * t + lax.broadcasted_iota(jnp.int32, (t, 1), 0) + 1
    means = []
    for gi, w in enumerate(POOL_WINDOWS):
        cnt = jnp.minimum(pos, w).astype(F32)
        means.append(sums[gi] / cnt)
    d = jnp.concatenate(means, axis=-1) - x
    o_ref[...] = _pool_project(d, pw_ref, ps_ref[...]).astype(BF16)
    buf_ref[0:pad, :] = x[t - pad:t, :]


def _pool_prompt(cp, pool_w, pool_scale):
    nt = SEQ // POOL_TILE
    t = POOL_TILE
    return pl.pallas_call(
        _pool_prompt_kernel,
        grid=(BATCH, nt),
        in_specs=[pl.BlockSpec((t, C_WIDTH), lambda b, n: (b * nt + n, 0)),
                  _const_spec((POOL_GROUPS, POOL_GROUP_DIM, POOL_GROUP_DIM)),
                  _const_spec((1, C_WIDTH))],
        out_specs=pl.BlockSpec((t, C_WIDTH), lambda b, n: (b * nt + n, 0)),
        out_shape=jax.ShapeDtypeStruct((N_PROMPT, C_WIDTH), BF16),
        scratch_shapes=[pltpu.VMEM((POOL_PAD + t, C_WIDTH), F32)],
        compiler_params=_params(("arbitrary", "arbitrary")),
        name="pool_prompt",
    )(cp, pool_w, pool_scale)


def _pool_sample_kernel(st_ref, x_ref, pw_ref, ps_ref, o_ref, nst_ref):
    x = x_ref[...]
    gd = POOL_GROUP_DIM
    means = []
    for gi, w in enumerate(POOL_WINDOWS):
        lo = gi * gd
        acc = x[:, lo:lo + gd]
        for j in range(1, w):
            acc = acc + st_ref[POOL_HIST - j][:, lo:lo + gd]
        means.append(acc / float(min(PAST_LEN + 1, w)))
    d = jnp.concatenate(means, axis=-1) - x
    o_ref[...] = _pool_project(d, pw_ref, ps_ref[...]).astype(BF16)
    for j in range(POOL_HIST - 1):
        nst_ref[j] = st_ref[j + 1]
    nst_ref[POOL_HIST - 1] = x


def _pool_sample(pool_st, cp, pool_w, pool_scale):
    return pl.pallas_call(
        _pool_sample_kernel,
        out_shape=[jax.ShapeDtypeStruct((DEC_BATCH, C_WIDTH), BF16),
                   jax.ShapeDtypeStruct((POOL_HIST, DEC_BATCH, C_WIDTH), F32)],
        compiler_params=_params(),
        name="pool_sample",
    )(pool_st, cp, pool_w, pool_scale)


def _layer_norm(y, g, b):
    mu = jnp.mean(y, axis=-1, keepdims=True)
    yc = y - mu
    var = jnp.mean(yc * yc, axis=-1, keepdims=True)
    return yc * lax.rsqrt(var + LN_EPS) * g + b


def _store_token_tiled(ref, val, rows):
    for s in range(ROW_TILES):
        ref[pl.ds(s, rows, stride=ROW_TILES), :] = val[:, s * LANES:(s + 1) * LANES]


def _load_token_tiled(ref, base, rows):
    return jnp.concatenate([ref[pl.ds(base + s, rows, stride=ROW_TILES), :] for s in range(ROW_TILES)], axis=-1)


def _merge_kernel(hp_ref, oap_ref, obp_ref, ocp_ref, gp_ref, hs_ref, oas_ref, obs_ref, ocs_ref, gs_ref,
                  wa_ref, wb_ref, wc_ref, wo_ref, lng_ref, lnb_ref, wr_ref, br_ref,
                  h1_ref, lg_ref, *, n_prompt_tiles):
    i = pl.program_id(0)

    def run(h_ref, oa_ref, ob_ref, oc_ref, g_ref):
        rows = h_ref.shape[0]
        merged = (g_ref[:, 0:D_MODEL].astype(F32) * _dot(oa_ref[...], wa_ref[...])
                  + g_ref[:, D_MODEL:2 * D_MODEL].astype(F32) * _dot(ob_ref[...], wb_ref[...])
                  + g_ref[:, 2 * D_MODEL:3 * D_MODEL].astype(F32) * _dot(oc_ref[...], wc_ref[...]))
        mix = _dot(merged.astype(BF16), wo_ref[...])
        h1 = _layer_norm(DN_ALPHA * h_ref[...] + mix, lng_ref[...], lnb_ref[...])
        h1_ref[0:rows, :] = h1
        hh, hm, _ = _split3(h1)
        wh, wm, _ = _split3(wr_ref[...])
        lg_ref[0:rows, :] = _dot(hh, wh) + _dot(hh, wm) + _dot(hm, wh) + br_ref[...]

    @pl.when(i < n_prompt_tiles)
    def _():
        run(hp_ref, oap_ref, obp_ref, ocp_ref, gp_ref)

    @pl.when(i == n_prompt_tiles)
    def _():
        run(hs_ref, oas_ref, obs_ref, ocs_ref, gs_ref)


def _merge(prompt, sample, wa, wb, wc, wo, lng, lnb, wr, br, tm):
    n_p = N_PROMPT // tm
    last = n_p - 1
    widths = (D_MODEL, A_WIDTH, B_WIDTH, C_WIDTH, N_BRANCHES * D_MODEL)
    in_specs = ([pl.BlockSpec((tm, w), lambda i: (jnp.minimum(i, last), 0)) for w in widths]
                + [_const_spec((DEC_BATCH, w)) for w in widths]
                + [_const_spec((A_WIDTH, D_MODEL)), _const_spec((B_WIDTH, D_MODEL)), _const_spec((C_WIDTH, D_MODEL)),
                   _const_spec((D_MODEL, D_MODEL)), _const_spec((1, D_MODEL)), _const_spec((1, D_MODEL)),
                   _const_spec((D_MODEL, LANES)), _const_spec((1, LANES))])
    return pl.pallas_call(
        functools.partial(_merge_kernel, n_prompt_tiles=n_p),
        grid=(n_p + 1,),
        in_specs=in_specs,
        out_specs=[pl.BlockSpec((tm, D_MODEL), lambda i: (i, 0)),
                   pl.BlockSpec((tm, LANES), lambda i: (i, 0))],
        out_shape=[jax.ShapeDtypeStruct((N_ROWS, D_MODEL), F32),
                   jax.ShapeDtypeStruct((N_ROWS, LANES), F32)],
        compiler_params=_params(("arbitrary",)),
        name="merge",
    )(*prompt, *sample, wa, wb, wc, wo, lng, lnb, wr, br)


ROUTE_TILE = 384
REC_EXPERT = 0
REC_RANK = 2
REC_WEIGHT = 4


def _route_kernel(lg_ref, rec_ref, sizes_ref, carry_ref):
    i = pl.program_id(0)
    t_rows = ROUTE_TILE

    @pl.when(i == 0)
    def _():
        carry_ref[...] = jnp.zeros_like(carry_ref)

    lg = lg_ref[...]
    lane = lax.broadcasted_iota(jnp.int32, (t_rows, LANES), 1).astype(F32)
    far = float(LANES)

    def masked_argmax(mask):
        v = jnp.max(jnp.where(mask, lg, NEG_INF), axis=-1, keepdims=True)
        idx = jnp.min(jnp.where(mask & (lg == v), lane, far), axis=-1, keepdims=True)
        return v, idx

    gmask = lane < N_GROUPS
    gmax, gsel = masked_argmax(gmask)
    gw = 1.0 / jnp.sum(jnp.where(gmask, jnp.exp(lg - gmax), 0.0), axis=-1, keepdims=True)
    lo = N_GROUPS + EXPERTS_PER_GROUP * gsel
    emask = (lane >= lo) & (lane < lo + EXPERTS_PER_GROUP)
    v1, i1 = masked_argmax(emask)
    v2, i2 = masked_argmax(emask & (lane != i1))
    t = jnp.exp(v2 - v1)
    w1 = gw / (1.0 + t)
    w2 = gw * t / (1.0 + t)
    e1 = i1 - N_GROUPS
    e2 = i2 - N_GROUPS

    o1 = jnp.where(lane == e1, 1.0, 0.0)
    o2 = jnp.where(lane == e2, 1.0, 0.0)
    r = lax.broadcasted_iota(jnp.int32, (t_rows, t_rows), 0)
    c = lax.broadcasted_iota(jnp.int32, (t_rows, t_rows), 1)
    before = jnp.where(c < r, 1.0, 0.0).astype(BF16)
    carry = carry_ref[...]
    tot1 = jnp.sum(o1, axis=0, keepdims=True)
    tot2 = jnp.sum(o2, axis=0, keepdims=True)
    rank1 = jnp.sum(o1 * (_dot(before, o1.astype(BF16)) + carry), axis=-1, keepdims=True)
    rank2 = jnp.sum(o2 * (_dot(before, o2.astype(BF16)) + (carry + tot1)), axis=-1, keepdims=True)
    carry = carry + tot1 + tot2
    carry_ref[...] = carry
    sizes_ref[...] = carry

    rec = jnp.zeros((t_rows, LANES), F32)
    for k, val in enumerate((e1, e2, rank1, rank2, w1, w2)):
        rec = jnp.where(lane == k, val, rec)
    rec_ref[...] = rec


def _route(logits):
    rec, sizes_row = pl.pallas_call(
        _route_kernel,
        grid=(N_ROWS // ROUTE_TILE,),
        in_specs=[pl.BlockSpec((ROUTE_TILE, LANES), lambda i: (i, 0))],
        out_specs=[pl.BlockSpec((ROUTE_TILE, LANES), lambda i: (i, 0)), _const_spec((1, LANES))],
        out_shape=[jax.ShapeDtypeStruct((N_ROWS, LANES), F32), jax.ShapeDtypeStruct((1, LANES), F32)],
        scratch_shapes=[pltpu.VMEM((1, LANES), F32)],
        compiler_params=_params(("arbitrary",)),
        name="route",
    )(logits)
    eid = rec[:, REC_EXPERT:REC_EXPERT + TOP_K].astype(jnp.int32)
    rank = rec[:, REC_RANK:REC_RANK + TOP_K].astype(jnp.int32)
    sizes = sizes_row[0, :N_EXPERTS].astype(jnp.int32)
    padded = (sizes + MOE_ROWS - 1) // MOE_ROWS * MOE_ROWS
    end_padded = jnp.cumsum(padded).astype(jnp.int32)
    start_padded = end_padded - padded
    experts = jnp.arange(N_EXPERTS, dtype=jnp.int32)
    dest = jnp.sum(jnp.where(eid[:, :, None] == experts, start_padded, 0), axis=-1) + rank
    blk_start = jnp.arange(MOE_BLOCKS, dtype=jnp.int32) * MOE_ROWS
    n_used = end_padded[-1] // MOE_ROWS
    blk_exp = jnp.sum((blk_start[:, None] >= end_padded[None, :]).astype(jnp.int32), axis=-1)
    last_exp = jnp.sum((blk_start[jnp.maximum(n_used - 1, 0)] >= end_padded).astype(jnp.int32))
    blk_exp = jnp.where(blk_start < end_padded[-1], blk_exp, last_exp).astype(jnp.int32)
    return rec, dest, end_padded, padded.astype(jnp.int32), blk_exp, n_used.reshape(1).astype(jnp.int32)


DMA_UNROLL = 16


def _tile_rows(ref, j):
    return ref.at[pl.ds(pl.multiple_of(j * ROW_TILES, ROW_TILES), ROW_TILES), :]


def _gather_rows(idx_ref, src_ref, dst_ref, sem, count):
    def body(jj, carry):
        for p in range(DMA_UNROLL):
            j = DMA_UNROLL * jj + p
            pltpu.make_async_copy(_tile_rows(src_ref, idx_ref[0, 0, j]), _tile_rows(dst_ref, j),
                                  sem).start(priority=p % 2)
        return carry
    lax.fori_loop(0, count // DMA_UNROLL, body, 0)


def _gather_wait(src_ref, dst_ref, sem, count):
    pltpu.make_async_copy(src_ref.at[pl.ds(0, count * ROW_TILES), :], dst_ref, sem).wait()


DISPATCH_TILE = 384


def _dispatch_kernel(endp_ref, padded_ref, nused_ref, idx_ref, h_ref, xb_hbm, stage_ref, zero_ref, sem_ref, zsem_ref):
    i = pl.program_id(0)
    n = pl.num_programs(0)
    tm = h_ref.shape[0]
    slot = i % 2
    blk = MOE_ROWS * ROW_TILES

    def zero_copy(b0):
        return pltpu.make_async_copy(zero_ref, xb_hbm.at[pl.ds(pl.multiple_of(b0 * ROW_TILES, ROW_TILES), blk), :], zsem_ref.at[0])

    @pl.when(i == 0)
    def _():
        zero_ref[...] = jnp.zeros_like(zero_ref)
        for e in range(N_EXPERTS):
            @pl.when(padded_ref[e] > 0)
            def _():
                zero_copy(endp_ref[e] - MOE_ROWS).start()

        def tail_start(b, carry):
            zero_copy(b * MOE_ROWS).start()
            return carry
        lax.fori_loop(nused_ref[0], MOE_BLOCKS, tail_start, 0)
        for e in range(N_EXPERTS):
            @pl.when(padded_ref[e] > 0)
            def _():
                zero_copy(endp_ref[e] - MOE_ROWS).wait()

        def tail_wait(b, carry):
            zero_copy(b * MOE_ROWS).wait()
            return carry
        lax.fori_loop(nused_ref[0], MOE_BLOCKS, tail_wait, 0)

    def wait_slot(s):
        for _ in range(TOP_K):
            pltpu.make_async_copy(stage_ref.at[s], xb_hbm.at[pl.ds(0, tm * ROW_TILES), :], sem_ref.at[s]).wait()

    @pl.when(i >= 2)
    def _():
        wait_slot(slot)

    stage = stage_ref.at[slot]
    _store_token_tiled(stage, h_ref[...], tm)

    def body(tt, carry):
        for u in range(DMA_UNROLL // TOP_K):
            t = (DMA_UNROLL // TOP_K) * tt + u
            for k in range(TOP_K):
                pltpu.make_async_copy(_tile_rows(stage, t), _tile_rows(xb_hbm, idx_ref[0, 0, k * tm + t]),
                                      sem_ref.at[slot]).start(priority=k)
        return carry
    lax.fori_loop(0, tm // (DMA_UNROLL // TOP_K), body, 0)

    @pl.when(i == n - 1)
    def _():
        wait_slot(slot)

        @pl.when(n > 1)
        def _():
            wait_slot(1 - slot)


def _tile_major_idx(dest, n_tiles, tm):
    idx = dest.reshape(n_tiles, tm, TOP_K)
    return jnp.transpose(idx, (0, 2, 1)).reshape(n_tiles, 1, TOP_K * tm)


def _dispatch(end_padded, padded, n_used, dest, h1):
    tm = DISPATCH_TILE
    n_tiles = N_ROWS // tm
    idx = _tile_major_idx(dest, n_tiles, tm)
    grid_spec = pltpu.PrefetchScalarGridSpec(
        num_scalar_prefetch=3,
        grid=(n_tiles,),
        in_specs=[pl.BlockSpec((1, 1, TOP_K * tm), lambda i, *_: (i, 0, 0), memory_space=pltpu.SMEM),
                  pl.BlockSpec((tm, D_MODEL), lambda i, *_: (i, 0))],
        out_specs=pl.BlockSpec(memory_space=pl.ANY),
        scratch_shapes=[pltpu.VMEM((2, tm * ROW_TILES, LANES), F32),
                        pltpu.VMEM((MOE_ROWS * ROW_TILES, LANES), F32),
                        pltpu.SemaphoreType.DMA((2,)),
                        pltpu.SemaphoreType.DMA((1,))])
    return pl.pallas_call(
        _dispatch_kernel,
        grid_spec=grid_spec,
        out_shape=jax.ShapeDtypeStruct((MOE_CAP * ROW_TILES, LANES), F32),
        compiler_params=_params(("arbitrary",)),
        name="dispatch",
    )(end_padded, padded, n_used, idx, h1)


def _expert_kernel(bexp_ref, nused_ref, x_ref, wg_ref, wu_ref, wd_ref, y_ref, wg16_ref, wu16_ref, wd16_ref):
    i = pl.program_id(0)
    n_used = nused_ref[0]
    rows = MOE_ROWS

    @pl.when((i == 0) | (bexp_ref[i] != bexp_ref[jnp.maximum(i - 1, 0)]))
    def _():
        wg16_ref[...] = wg_ref[...].astype(BF16)
        wu16_ref[...] = wu_ref[...].astype(BF16)
        wd16_ref[...] = wd_ref[...].astype(BF16)

    @pl.when(i < n_used)
    def _():
        x = _load_token_tiled(x_ref, 0, rows).astype(BF16)
        hid = _silu(_dot(x, wg16_ref[...])) * _dot(x, wu16_ref[...])
        _store_token_tiled(y_ref, _dot(hid.astype(BF16), wd16_ref[...]), rows)

    @pl.when(i >= n_used)
    def _():
        y_ref[...] = jnp.zeros_like(y_ref)


def _experts(layer, blk_exp, n_used, xb, w_gate, w_up, w_down):
    rows = MOE_ROWS
    blk = (rows * ROW_TILES, LANES)
    grid_spec = pltpu.PrefetchScalarGridSpec(
        num_scalar_prefetch=2,
        grid=(MOE_BLOCKS,),
        in_specs=[pl.BlockSpec(blk, lambda i, be, nu: (jnp.minimum(i, nu[0] - 1), 0)),
                  pl.BlockSpec((None, None, D_MODEL, EXPERT_FF), lambda i, be, nu: (layer, be[i], 0, 0)),
                  pl.BlockSpec((None, None, D_MODEL, EXPERT_FF), lambda i, be, nu: (layer, be[i], 0, 0)),
                  pl.BlockSpec((None, None, EXPERT_FF, D_MODEL), lambda i, be, nu: (layer, be[i], 0, 0))],
        out_specs=pl.BlockSpec(blk, lambda i, be, nu: (i, 0)),
        scratch_shapes=[pltpu.VMEM((D_MODEL, EXPERT_FF), BF16),
                        pltpu.VMEM((D_MODEL, EXPERT_FF), BF16),
                        pltpu.VMEM((EXPERT_FF, D_MODEL), BF16)])
    return pl.pallas_call(
        _expert_kernel,
        grid_spec=grid_spec,
        out_shape=jax.ShapeDtypeStruct((MOE_CAP * ROW_TILES, LANES), F32),
        compiler_params=_params(("arbitrary",)),
        name="experts",
    )(blk_exp, n_used, xb, w_gate, w_up, w_down)


def _combine_kernel(idx_ref, idxn_ref, h1_ref, rec_ref, y_hbm, lng_ref, lnb_ref, o_ref, ybuf_ref, sem_ref, *, n_tiles):
    i = pl.program_id(0)
    tm = h1_ref.shape[0]
    slot = i % 2
    count = TOP_K * tm

    @pl.when(i == 0)
    def _():
        _gather_rows(idx_ref, y_hbm, ybuf_ref.at[0], sem_ref.at[0], count)

    @pl.when(i + 1 < n_tiles)
    def _():
        _gather_rows(idxn_ref, y_hbm, ybuf_ref.at[1 - slot], sem_ref.at[1 - slot], count)

    _gather_wait(y_hbm, ybuf_ref.at[slot], sem_ref.at[slot], count)
    y = DN_ALPHA * h1_ref[...]
    for k in range(TOP_K):
        wk = rec_ref[:, REC_WEIGHT + k:REC_WEIGHT + k + 1]
        y = y + wk * _load_token_tiled(ybuf_ref.at[slot], k * tm * ROW_TILES, tm)
    o_ref[...] = _layer_norm(y, lng_ref[...], lnb_ref[...])


def _combine(dest, rec, h1, yb, lng, lnb, tm, row0, n):
    n_tiles = n // tm
    blk0 = row0 // tm
    idx = _tile_major_idx(dest[row0:row0 + n], n_tiles, tm)
    last = n_tiles - 1
    return pl.pallas_call(
        functools.partial(_combine_kernel, n_tiles=n_tiles),
        grid=(n_tiles,),
        in_specs=[pl.BlockSpec((1, 1, TOP_K * tm), lambda i: (i, 0, 0), memory_space=pltpu.SMEM),
                  pl.BlockSpec((1, 1, TOP_K * tm), lambda i: (jnp.minimum(i + 1, last), 0, 0),
                               memory_space=pltpu.SMEM),
                  pl.BlockSpec((tm, D_MODEL), lambda i: (blk0 + i, 0)),
                  pl.BlockSpec((tm, LANES), lambda i: (blk0 + i, 0)),
                  pl.BlockSpec(memory_space=pl.ANY),
                  _const_spec((1, D_MODEL)), _const_spec((1, D_MODEL))],
        out_specs=pl.BlockSpec((tm, D_MODEL), lambda i: (i, 0)),
        out_shape=jax.ShapeDtypeStruct((n, D_MODEL), F32),
        scratch_shapes=[pltpu.VMEM((2, TOP_K * tm * ROW_TILES, LANES), F32),
                        pltpu.SemaphoreType.DMA((2,))],
        compiler_params=_params(("arbitrary",)),
        name="combine",
    )(idx, idx, h1, rec, yb, lng, lnb)


PROMPT_TILE = 512
COMBINE_TILE = 256


def _row(v, width=None):
    v = v.reshape(1, -1).astype(F32)
    if width is not None and v.shape[1] < width:
        v = jnp.pad(v, ((0, 0), (0, width - v.shape[1])))
    return v


def _layer(layer, hp, hs, ck, cv, conv_st, rec_all, pool_st, lw, w_packed, experts_w):
    alog_row = _row(lw['gdn_a_log'], LANES)
    dtb_row = _row(lw['gdn_dt_bias'], LANES)
    norm_g = _row(lw['gdn_norm_g'])
    pool_scale = _row(lw['pool_scale'])
    sinks = lw['attn_sinks'].astype(F32)
    kvw = A_KV_HEADS * A_HEAD_DIM

    gates_p, bqkv_p, aq_p, z_p, cp_p, kv_p, ab_p = _in_proj(layer, hp, w_packed, PROMPT_TILE)
    oa_p = _attn_prompt(sinks, aq_p, kv_p)
    ob_p, rec_p = _gdn_prompt(bqkv_p, z_p, ab_p, lw['gdn_conv_w'], alog_row, dtb_row, norm_g)
    oc_p = _pool_prompt(cp_p, lw['pool_w'], pool_scale)
    kv_tail = kv_p.reshape(BATCH, SEQ, 2, A_KV_HEADS, A_HEAD_DIM)[:, SEQ - WINDOW:]
    new_k_p, new_v_p = kv_tail[:, :, 0], kv_tail[:, :, 1]
    new_conv_p = bqkv_p.reshape(BATCH, SEQ, 3 * B_WIDTH)[:, SEQ - (CONV_W - 1):]
    new_pool_p = cp_p.reshape(BATCH, SEQ, C_WIDTH)[:, SEQ - POOL_HIST:]

    gates_s, bqkv_s, aq_s, z_s, cp_s, kv_s, ab_s = _in_proj(layer, hs, w_packed, DEC_BATCH)
    oa_s3, nk_s, nv_s = _attn_sample(layer, sinks, aq_s.astype(F32).reshape(DEC_BATCH, A_HEADS, A_HEAD_DIM),
                                     kv_s[:, :kvw], kv_s[:, kvw:], ck, cv)
    oa_s = oa_s3.reshape(DEC_BATCH, A_WIDTH).astype(BF16)
    ob_s, nconv_s, rec_s = _gdn_sample(layer, jnp.transpose(conv_st, (1, 0, 2)), bqkv_s, z_s, ab_s, lw['gdn_conv_w'],
                                       alog_row, dtb_row, norm_g, rec_all)
    oc_s, npool_s = _pool_sample(jnp.transpose(pool_st, (1, 0, 2)), cp_s, lw['pool_w'], pool_scale)
    new_k_s = nk_s.reshape(DEC_BATCH, WINDOW, A_KV_HEADS, A_HEAD_DIM)
    new_v_s = nv_s.reshape(DEC_BATCH, WINDOW, A_KV_HEADS, A_HEAD_DIM)
    new_conv_s = jnp.transpose(nconv_s, (1, 0, 2))
    new_pool_s = jnp.transpose(npool_s, (1, 0, 2))

    wa, wb, wc = (lw[k].astype(BF16) for k in ('w_branch_a', 'w_branch_b', 'w_branch_c'))
    wo = lw['w_o'].astype(BF16)
    wr = jnp.concatenate([lw['router_group_w'],
                          jnp.transpose(lw['router_expert_w'], (1, 0, 2)).reshape(D_MODEL, N_EXPERTS),
                          jnp.zeros((D_MODEL, LANES - ROUTER_COLS), F32)], axis=-1)
    br = _row(jnp.concatenate([lw['router_group_b'], lw['router_expert_b'].reshape(-1)]), LANES)
    ln1g, ln1b, ln2g, ln2b = (_row(lw[k]) for k in ('ln1_g', 'ln1_b', 'ln2_g', 'ln2_b'))
    h1, logits = _merge((hp, oa_p, ob_p, oc_p, gates_p), (hs, oa_s, ob_s, oc_s, gates_s),
                        wa, wb, wc, wo, ln1g, ln1b, wr, br, PROMPT_TILE)
    rec, dest, end_padded, padded, blk_exp, n_used = _route(logits)
    xb = _dispatch(end_padded, padded, n_used, dest, h1)
    yb = _experts(layer, blk_exp, n_used, xb, *experts_w)
    h2_p = _combine(dest, rec, h1, yb, ln2g, ln2b, COMBINE_TILE, 0, N_PROMPT)
    h2_s = _combine(dest, rec, h1, yb, ln2g, ln2b, DEC_BATCH, N_PROMPT, DEC_BATCH)
    return (h2_p, h2_s, (new_k_p, new_v_p, new_conv_p, rec_p, new_pool_p),
            (new_k_s, new_v_s, new_conv_s, rec_s, new_pool_s))


def kernel(x_prompt, x_sample, cache_attn_k, cache_attn_v, state_gdn_conv, state_gdn_rec, state_pool, w_in, attn_sinks, gdn_conv_w, gdn_a_log, gdn_dt_bias, gdn_norm_g, pool_w, pool_scale, w_branch_a, w_branch_b, w_branch_c, w_o, ln1_g, ln1_b, router_group_w, router_group_b, router_expert_w, router_expert_b, w_gate, w_up, w_down, ln2_g, ln2_b):
    weights = dict(attn_sinks=attn_sinks, gdn_conv_w=gdn_conv_w, gdn_a_log=gdn_a_log,
                   gdn_dt_bias=gdn_dt_bias, gdn_norm_g=gdn_norm_g, pool_w=pool_w, pool_scale=pool_scale,
                   w_branch_a=w_branch_a, w_branch_b=w_branch_b, w_branch_c=w_branch_c, w_o=w_o,
                   ln1_g=ln1_g, ln1_b=ln1_b, router_group_w=router_group_w, router_group_b=router_group_b,
                   router_expert_w=router_expert_w, router_expert_b=router_expert_b,
                   ln2_g=ln2_g, ln2_b=ln2_b)
    hp = x_prompt.reshape(N_PROMPT, D_MODEL)
    hs = x_sample.reshape(DEC_BATCH, D_MODEL)
    kvw = A_KV_HEADS * A_HEAD_DIM
    ck = cache_attn_k.reshape(DEPTH, DEC_BATCH, WINDOW, kvw)
    cv = cache_attn_v.reshape(DEPTH, DEC_BATCH, WINDOW, kvw)
    w_packed = _pack_w_in(w_in)
    st_p = [[] for _ in range(5)]
    st_s = [[] for _ in range(5)]
    for l in range(DEPTH):
        lw = {k: v[l] for k, v in weights.items()}
        hp, hs, sp, ss = _layer(l, hp, hs, ck, cv, state_gdn_conv[l], state_gdn_rec, state_pool[l], lw,
                                w_packed, (w_gate, w_up, w_down))
        for j in range(5):
            st_p[j].append(sp[j])
            st_s[j].append(ss[j])
    return (hp.reshape(BATCH, SEQ, D_MODEL), hs.reshape(DEC_BATCH, 1, D_MODEL),
            *(jnp.stack(t) for t in st_p), *(jnp.stack(t) for t in st_s))
```

```python
import functools

import jax
import jax.numpy as jnp
import numpy as np
from jax import lax
from jax.experimental import pallas as pl
from jax.experimental.pallas import tpu as pltpu

F32 = jnp.float32
BF16 = jnp.bfloat16

D_MODEL = 1024
BATCH = 2
SEQ = 8192
DEPTH = 2
DEC_BATCH = 128
PAST_LEN = 16384
WINDOW = 128
A_HEADS = 8
A_KV_HEADS = 2
A_HEAD_DIM = 64
A_GROUP = A_HEADS // A_KV_HEADS
A_WIDTH = A_HEADS * A_HEAD_DIM
B_HEADS = 4
B_HEAD_DIM = 128
B_WIDTH = B_HEADS * B_HEAD_DIM
CONV_W = 4
GDN_CHUNK = 64
POOL_WINDOWS = (2, 4, 8, 16)
POOL_GROUPS = 4
C_WIDTH = 512
POOL_GROUP_DIM = C_WIDTH // POOL_GROUPS
POOL_HIST = max(POOL_WINDOWS) - 1
N_BRANCHES = 3
IN_SPLIT_SIZES = (A_WIDTH, A_KV_HEADS * A_HEAD_DIM, A_KV_HEADS * A_HEAD_DIM, 3 * B_WIDTH,
                  B_WIDTH, B_HEADS, B_HEADS, C_WIDTH, N_BRANCHES * D_MODEL)
N_GROUPS = 4
EXPERTS_PER_GROUP = 8
N_EXPERTS = N_GROUPS * EXPERTS_PER_GROUP
TOP_K = 2
EXPERT_FF = 512
DN_ALPHA = (2.0 * DEPTH) ** 0.25
LN_EPS = 1e-5
RMS_EPS = 1e-6
NEG_INF = -1e30

LANES = 128
SUBLANES = 8
ROW_TILES = D_MODEL // LANES
VMEM_LIMIT = 56 * 1024 * 1024

N_PROMPT = BATCH * SEQ
N_ROWS = N_PROMPT + DEC_BATCH

SEG_GATES = (0, N_BRANCHES * D_MODEL)
SEG_BQKV = (3072, 3 * B_WIDTH)
SEG_AQ = (4608, A_WIDTH)
SEG_Z = (5120, B_WIDTH)
SEG_CP = (5632, C_WIDTH)
SEG_KV = (6144, 2 * A_KV_HEADS * A_HEAD_DIM)
SEG_AB = (6400, LANES)
IN_PACKED = 6528

MOE_ROWS = 256
MOE_ASSIGN = N_ROWS * TOP_K
MOE_CAP = -(-MOE_ASSIGN // MOE_ROWS) * MOE_ROWS + N_EXPERTS * MOE_ROWS
MOE_BLOCKS = MOE_CAP // MOE_ROWS
ROUTER_COLS = N_GROUPS + N_EXPERTS


def _sigmoid(x):
    return 1.0 / (1.0 + jnp.exp(-x))


def _silu(x):
    return x * _sigmoid(x)


def _dot(a, b):
    return jnp.dot(a, b, preferred_element_type=F32)


def _dot_nt(a, b):
    return lax.dot_general(a, b, (((1,), (1,)), ((), ())), preferred_element_type=F32)


def _dot_tn(a, b):
    return lax.dot_general(a, b, (((0,), (0,)), ((), ())), preferred_element_type=F32)


def _split3(x):
    hi = x.astype(BF16)
    r = x - hi.astype(F32)
    mid = r.astype(BF16)
    lo = (r - mid.astype(F32)).astype(BF16)
    return hi, mid, lo


def _params(sem=None):
    return pltpu.CompilerParams(dimension_semantics=sem, vmem_limit_bytes=VMEM_LIMIT)


def _const_spec(shape):
    nd = len(shape)
    return pl.BlockSpec(shape, lambda *_: (0,) * nd)


def _pack_w_in(w):
    w = w.astype(BF16)
    aq, ak, av, bqkv, bz, ba, bb, cp, gate = jnp.split(w, np.cumsum(IN_SPLIT_SIZES)[:-1].tolist(), axis=-1)
    aq = aq * (A_HEAD_DIM ** -0.5)
    ab = jnp.concatenate([ba, bb, jnp.zeros(w.shape[:-1] + (LANES - 2 * B_HEADS,), w.dtype)], -1)
    return jnp.concatenate([gate, bqkv, aq, bz, cp, ak, av, ab], -1)


def _in_proj_kernel(x_ref, w_ref, gates_ref, bqkv_ref, aq_ref, z_ref, cp_ref, kv_ref, ab_ref):
    x = x_ref[...].astype(BF16)

    def mm(lo, n):
        return _dot(x, w_ref[:, lo:lo + n])

    step = 512
    for c in range(SEG_GATES[1] // step):
        gates_ref[:, c * step:(c + 1) * step] = _sigmoid(mm(SEG_GATES[0] + c * step, step)).astype(BF16)
    for c in range(SEG_BQKV[1] // step):
        bqkv_ref[:, c * step:(c + 1) * step] = mm(SEG_BQKV[0] + c * step, step)
    aq_ref[...] = mm(*SEG_AQ).astype(BF16)
    z_ref[...] = mm(*SEG_Z)
    cp_ref[...] = mm(*SEG_CP)
    kv_ref[...] = mm(*SEG_KV)
    ab_ref[...] = mm(*SEG_AB)


def _in_proj(layer, x, w_packed, tm):
    n = x.shape[0]
    widths = (SEG_GATES[1], SEG_BQKV[1], SEG_AQ[1], SEG_Z[1], SEG_CP[1], SEG_KV[1], SEG_AB[1])
    dtypes = (BF16, F32, BF16, F32, F32, F32, F32)
    return pl.pallas_call(
        _in_proj_kernel,
        grid=(n // tm,),
        in_specs=[pl.BlockSpec((tm, D_MODEL), lambda i: (i, 0)),
                  pl.BlockSpec((None, D_MODEL, IN_PACKED), lambda i: (layer, 0, 0), pipeline_mode=pl.Buffered(1))],
        out_specs=[pl.BlockSpec((tm, w), lambda i: (i, 0)) for w in widths],
        out_shape=[jax.ShapeDtypeStruct((n, w), dt) for w, dt in zip(widths, dtypes)],
        compiler_params=_params(("arbitrary",)),
        name="in_proj",
    )(x, w_packed)


def _alibi_slope(h):
    return 2.0 ** (-8.0 * (h + 1) / A_HEADS)


def _attn_prompt_kernel(sink_ref, q_ref, kvc_ref, kvp_ref, o_ref):
    n = pl.program_id(1)
    kvc = kvc_ref[...]
    kvp = kvp_ref[...]
    t = lax.broadcasted_iota(jnp.int32, (WINDOW, 2 * WINDOW), 0)
    s = lax.broadcasted_iota(jnp.int32, (WINDOW, 2 * WINDOW), 1)
    dist = WINDOW + t - s
    valid = (dist >= 0) & (dist <= WINDOW) & ((s >= WINDOW) | (n > 0))
    distf = dist.astype(F32)
    hd = A_HEAD_DIM
    vo = A_KV_HEADS * hd
    outs = []
    for kh in range(A_KV_HEADS):
        k = jnp.concatenate([kvp[:, kh * hd:(kh + 1) * hd], kvc[:, kh * hd:(kh + 1) * hd]], axis=0).astype(BF16)
        v = jnp.concatenate([kvp[:, vo + kh * hd:vo + (kh + 1) * hd],
                             kvc[:, vo + kh * hd:vo + (kh + 1) * hd]], axis=0).astype(BF16)
        for g in range(A_GROUP):
            h = kh * A_GROUP + g
            sc = _dot_nt(q_ref[:, h * hd:(h + 1) * hd], k)
            sc = jnp.where(valid, sc - _alibi_slope(h) * distf, NEG_INF)
            sink = sink_ref[h]
            m = jnp.maximum(jnp.max(sc, axis=-1, keepdims=True), sink)
            p = jnp.exp(sc - m)
            denom = jnp.sum(p, axis=-1, keepdims=True) + jnp.exp(sink - m)
            outs.append(_dot(p.astype(BF16), v) / denom)
    o_ref[...] = jnp.concatenate(outs, axis=-1).astype(BF16)


def _attn_prompt(sinks, aq, kv):
    nb = SEQ // WINDOW
    return pl.pallas_call(
        _attn_prompt_kernel,
        grid=(BATCH, nb),
        in_specs=[pl.BlockSpec(memory_space=pltpu.SMEM),
                  pl.BlockSpec((WINDOW, A_WIDTH), lambda b, n: (b * nb + n, 0)),
                  pl.BlockSpec((WINDOW, SEG_KV[1]), lambda b, n: (b * nb + n, 0)),
                  pl.BlockSpec((WINDOW, SEG_KV[1]), lambda b, n: (b * nb + jnp.maximum(n - 1, 0), 0))],
        out_specs=pl.BlockSpec((WINDOW, A_WIDTH), lambda b, n: (b * nb + n, 0)),
        out_shape=jax.ShapeDtypeStruct((N_PROMPT, A_WIDTH), BF16),
        compiler_params=_params(("arbitrary", "arbitrary")),
        name="attn_prompt",
    )(sinks, aq, kv, kv)


ATTN_S_BLOCK = 8


def _attn_sample_kernel(sink_ref, q_ref, kn_ref, vn_ref, ck_ref, cv_ref, o_ref, nk_ref, nv_ref):
    hd = A_HEAD_DIM
    row = lax.broadcasted_iota(jnp.int32, (A_HEADS, 1), 0)
    slope = jnp.zeros((A_HEADS, 1), F32)
    sink = jnp.zeros((A_HEADS, 1), F32)
    for h in range(A_HEADS):
        slope = jnp.where(row == h, _alibi_slope(h), slope)
        sink = jnp.where(row == h, sink_ref[h], sink)
    pos = lax.broadcasted_iota(jnp.int32, (A_HEADS, WINDOW), 1)
    bias = -slope * (WINDOW - pos).astype(F32)
    first = row < A_GROUP
    for b in range(ATTN_S_BLOCK):
        kc = ck_ref[b]
        vc = cv_ref[b]
        qf = q_ref[b]
        q = qf.astype(BF16)
        kn = kn_ref[b:b + 1, :]
        vn = vn_ref[b:b + 1, :]
        s0 = _dot_nt(q, kc[:, 0:hd].astype(BF16))
        s1 = _dot_nt(q, kc[:, hd:2 * hd].astype(BF16))
        sc = jnp.where(first, s0, s1) + bias
        sn0 = jnp.sum(qf * kn[:, 0:hd], axis=-1, keepdims=True)
        sn1 = jnp.sum(qf * kn[:, hd:2 * hd], axis=-1, keepdims=True)
        sn = jnp.where(first, sn0, sn1)
        m = jnp.maximum(jnp.maximum(jnp.max(sc, axis=-1, keepdims=True), sn), sink)
        p = jnp.exp(sc - m)
        pn = jnp.exp(sn - m)
        denom = jnp.sum(p, axis=-1, keepdims=True) + pn + jnp.exp(sink - m)
        pb = p.astype(BF16)
        o0 = _dot(pb, vc[:, 0:hd].astype(BF16)) + pn * vn[:, 0:hd]
        o1 = _dot(pb, vc[:, hd:2 * hd].astype(BF16)) + pn * vn[:, hd:2 * hd]
        o_ref[b] = jnp.where(first, o0, o1) / denom
        nk_ref[b, 0:WINDOW - 1, :] = kc[1:WINDOW, :]
        nk_ref[b, WINDOW - 1:WINDOW, :] = kn
        nv_ref[b, 0:WINDOW - 1, :] = vc[1:WINDOW, :]
        nv_ref[b, WINDOW - 1:WINDOW, :] = vn


def _attn_sample(layer, sinks, q3, kn, vn, ck, cv):
    bs = ATTN_S_BLOCK
    kvw = A_KV_HEADS * A_HEAD_DIM
    return pl.pallas_call(
        _attn_sample_kernel,
        grid=(DEC_BATCH // bs,),
        in_specs=[pl.BlockSpec(memory_space=pltpu.SMEM),
                  pl.BlockSpec((bs, A_HEADS, A_HEAD_DIM), lambda i: (i, 0, 0)),
                  pl.BlockSpec((bs, kvw), lambda i: (i, 0)),
                  pl.BlockSpec((bs, kvw), lambda i: (i, 0)),
                  pl.BlockSpec((None, bs, WINDOW, kvw), lambda i: (layer, i, 0, 0)),
                  pl.BlockSpec((None, bs, WINDOW, kvw), lambda i: (layer, i, 0, 0))],
        out_specs=[pl.BlockSpec((bs, A_HEADS, A_HEAD_DIM), lambda i: (i, 0, 0)),
                   pl.BlockSpec((bs, WINDOW, kvw), lambda i: (i, 0, 0)),
                   pl.BlockSpec((bs, WINDOW, kvw), lambda i: (i, 0, 0))],
        out_shape=[jax.ShapeDtypeStruct((DEC_BATCH, A_HEADS, A_HEAD_DIM), F32),
                   jax.ShapeDtypeStruct((DEC_BATCH, WINDOW, kvw), F32),
                   jax.ShapeDtypeStruct((DEC_BATCH, WINDOW, kvw), F32)],
        compiler_params=_params(("arbitrary",)),
        name="attn_sample",
    )(sinks, q3, kn, vn, ck, cv)


GDN_TILE = 512
GDN_STACK = B_HEADS * GDN_CHUNK


def _gdn_gates(ab, alog, dtb):
    xs = ab + dtb
    softplus = jnp.maximum(xs, 0.0) + jnp.log(1.0 + jnp.exp(-jnp.abs(xs)))
    g = -jnp.exp(alog) * softplus
    beta = _sigmoid(ab)
    return g, beta


def _l2norm(x):
    return x * lax.rsqrt(jnp.sum(x * x, axis=-1, keepdims=True) + RMS_EPS)


def _gdn_out_norm(o, z, norm_g):
    return o * lax.rsqrt(jnp.mean(o * o, axis=-1, keepdims=True) + RMS_EPS) * norm_g * _silu(z)


def _gdn_prompt_kernel(x_ref, z_ref, ab_ref, cw_ref, alog_ref, dtb_ref, ng_ref,
                       o_ref, sfin_ref, xbuf_ref, s_ref):
    n = pl.program_id(1)
    t_rows = GDN_TILE
    c_rows = GDN_CHUNK
    hd = B_HEAD_DIM
    pad = SUBLANES

    @pl.when(n == 0)
    def _():
        xbuf_ref[0:pad, :] = jnp.zeros((pad, 3 * B_WIDTH), F32)
        s_ref[...] = jnp.zeros_like(s_ref)

    x = x_ref[...]
    xbuf_ref[pad:pad + t_rows, :] = x
    cw = cw_ref[...]
    conv = x * cw[CONV_W - 1:CONV_W, :]
    for j in range(CONV_W - 1):
        back = CONV_W - 1 - j
        conv = conv + xbuf_ref[pad - back:pad - back + t_rows, :] * cw[j:j + 1, :]
    xbuf_ref[0:pad, :] = x[t_rows - pad:t_rows, :]
    conv = _silu(conv)

    g_all, beta_all = _gdn_gates(ab_ref[...], alog_ref[...], dtb_ref[...])
    ri = lax.broadcasted_iota(jnp.int32, (t_rows, t_rows), 0)
    ci = lax.broadcasted_iota(jnp.int32, (t_rows, t_rows), 1)
    tri = jnp.where((ri // c_rows == ci // c_rows) & (ri >= ci), 1.0, 0.0).astype(BF16)
    g_hi, g_mid, g_lo = _split3(g_all)
    gcum_all = _dot(tri, g_hi) + _dot(tri, g_mid) + _dot(tri, g_lo)

    qn, kn, vv = [], [], []
    for h in range(B_HEADS):
        qn.append(_l2norm(conv[:, h * hd:(h + 1) * hd]) * (hd ** -0.5))
        kn.append(_l2norm(conv[:, B_WIDTH + h * hd:B_WIDTH + (h + 1) * hd]))
        vv.append(conv[:, 2 * B_WIDTH + h * hd:2 * B_WIDTH + (h + 1) * hd])

    st = GDN_STACK
    nc = t_rows // c_rows
    r = lax.broadcasted_iota(jnp.int32, (st, st), 0)
    c = lax.broadcasted_iota(jnp.int32, (st, st), 1)
    same = (r // c_rows) == (c // c_rows)
    incl = same & (r >= c)
    strict = same & (r > c)

    def chunk_rows(a, ck):
        return a[ck * c_rows:(ck + 1) * c_rows]

    def stack(parts):
        return jnp.stack([jnp.concatenate([chunk_rows(p, ck) for p in parts], axis=0) for ck in range(nc)], axis=0)

    def stack_col(a, lane0):
        return stack([a[:, lane0 + h:lane0 + h + 1] for h in range(B_HEADS)])

    qs, ks, vs = stack(qn), stack(kn), stack(vv)
    gcol = stack_col(gcum_all, 0)
    bcol = stack_col(beta_all, B_HEADS)
    glast = [[gcum_all[(ck + 1) * c_rows - 1:(ck + 1) * c_rows, h:h + 1] for h in range(B_HEADS)] for ck in range(nc)]
    glast_col = jnp.stack([jnp.concatenate([jnp.broadcast_to(gl, (c_rows, 1)) for gl in glast[ck]], axis=0)
                           for ck in range(nc)], axis=0)
    gt_hi, gt_mid, gt_lo = _split3(g_all.T)
    tri_t = jnp.where((ri // c_rows == ci // c_rows) & (ri <= ci), 1.0, 0.0).astype(BF16)
    gcum_t = _dot(gt_hi, tri_t) + _dot(gt_mid, tri_t) + _dot(gt_lo, tri_t)
    grow = jnp.stack([jnp.concatenate([gcum_t[h:h + 1, ck * c_rows:(ck + 1) * c_rows] for h in range(B_HEADS)], axis=1)
                      for ck in range(nc)], axis=0)

    def bmm(a, b):
        return jnp.einsum('cij,cjk->cik', a, b, preferred_element_type=F32)

    def bmm_nt(a, b):
        return jnp.einsum('cik,cjk->cij', a, b, preferred_element_type=F32)

    decay = jnp.where(incl, jnp.exp(jnp.where(incl, gcol - grow, 0.0)), 0.0)
    exp_g = jnp.exp(gcol)
    kb = ks * bcol
    vb = vs * bcol
    ks16 = ks.astype(BF16)
    kk = bmm_nt(kb.astype(BF16), ks16)
    qk = bmm_nt(qs.astype(BF16), ks16)
    m_mat = jnp.where(strict, kk * decay, 0.0)
    m16 = m_mat.astype(BF16)
    zero16 = jnp.zeros((), BF16)
    p16 = jnp.where(r == c, 1.0, 0.0).astype(BF16) - jnp.where((r // 2) == (c // 2), m16, zero16)
    blk = 2
    while blk < c_rows:
        lvl = ((r // (2 * blk)) == (c // (2 * blk))) & ((r // blk) != (c // blk))
        a_off = jnp.where(lvl, m16, zero16)
        p16 = p16 - bmm(bmm(p16, a_off).astype(BF16), p16).astype(BF16)
        blk *= 2
    uw = bmm(p16, jnp.concatenate([vb, kb * exp_g], axis=2).astype(BF16))
    aqk = (qk * decay).astype(BF16)
    qg = qs * exp_g
    kd = (ks * jnp.exp(glast_col - gcol)).astype(BF16)

    o_chunks = []
    for ck in range(nc):
        vnew, qs_s = [], []
        for h in range(B_HEADS):
            hs = slice(h * c_rows, (h + 1) * c_rows)
            wq = jnp.concatenate([uw[ck, hs, hd:2 * hd], qg[ck, hs]], axis=0).astype(BF16)
            rs = _dot(wq, s_ref[h].astype(BF16))
            vnew.append(uw[ck, hs, 0:hd] - rs[0:c_rows])
            qs_s.append(rs[c_rows:2 * c_rows])
        vnew16 = jnp.concatenate(vnew, axis=0).astype(BF16)
        o_st = jnp.concatenate(qs_s, axis=0) + _dot(aqk[ck], vnew16)
        for h in range(B_HEADS):
            hs = slice(h * c_rows, (h + 1) * c_rows)
            s_ref[h] = s_ref[h] * jnp.exp(glast[ck][h]) + _dot_tn(kd[ck, hs], vnew16[hs])
        o_chunks.append([o_st[h * c_rows:(h + 1) * c_rows] for h in range(B_HEADS)])

    ng = ng_ref[...]
    z = z_ref[...]
    outs = []
    for h in range(B_HEADS):
        o_h = jnp.concatenate([oc[h] for oc in o_chunks], axis=0)
        outs.append(_gdn_out_norm(o_h, z[:, h * hd:(h + 1) * hd], ng))
    o_ref[...] = jnp.concatenate(outs, axis=-1).astype(BF16)
    sfin_ref[0] = s_ref[...]


def _gdn_prompt(bqkv, z, ab, conv_w, alog_row, dtb_row, norm_g):
    nt = SEQ // GDN_TILE
    t = GDN_TILE
    return pl.pallas_call(
        _gdn_prompt_kernel,
        grid=(BATCH, nt),
        in_specs=[pl.BlockSpec((t, 3 * B_WIDTH), lambda b, n: (b * nt + n, 0)),
                  pl.BlockSpec((t, B_WIDTH), lambda b, n: (b * nt + n, 0)),
                  pl.BlockSpec((t, LANES), lambda b, n: (b * nt + n, 0)),
                  _const_spec((CONV_W, 3 * B_WIDTH)),
                  _const_spec((1, LANES)),
                  _const_spec((1, LANES)),
                  _const_spec((1, B_HEAD_DIM))],
        out_specs=[pl.BlockSpec((t, B_WIDTH), lambda b, n: (b * nt + n, 0)),
                   pl.BlockSpec((1, B_HEADS, B_HEAD_DIM, B_HEAD_DIM), lambda b, n: (b, 0, 0, 0))],
        out_shape=[jax.ShapeDtypeStruct((N_PROMPT, B_WIDTH), BF16),
                   jax.ShapeDtypeStruct((BATCH, B_HEADS, B_HEAD_DIM, B_HEAD_DIM), F32)],
        scratch_shapes=[pltpu.VMEM((SUBLANES + t, 3 * B_WIDTH), F32),
                        pltpu.VMEM((B_HEADS, B_HEAD_DIM, B_HEAD_DIM), F32)],
        compiler_params=_params(("arbitrary", "arbitrary")),
        name="gdn_prompt",
    )(bqkv, z, ab, conv_w, alog_row, dtb_row, norm_g)


GDN_S_BLOCK = 8


def _gdn_sample_kernel(st_ref, x_ref, z_ref, ab_ref, cw_ref, alog_ref, dtb_ref, ng_ref, s_ref,
                       o_ref, nst_ref, ns_ref):
    hd = B_HEAD_DIM
    x = x_ref[...]
    cw = cw_ref[...]
    conv = x * cw[CONV_W - 1:CONV_W, :]
    for j in range(CONV_W - 1):
        conv = conv + st_ref[j] * cw[j:j + 1, :]
    for j in range(CONV_W - 2):
        nst_ref[j] = st_ref[j + 1]
    nst_ref[CONV_W - 2] = x
    conv = _silu(conv)
    g_all, beta_all = _gdn_gates(ab_ref[...], alog_ref[...], dtb_ref[...])
    eg_all = jnp.exp(g_all)
    r = lax.broadcasted_iota(jnp.int32, (hd, hd), 0)
    c = lax.broadcasted_iota(jnp.int32, (hd, hd), 1)
    eye = r == c
    zpad = jnp.zeros((SUBLANES - 2, hd), F32)
    ng = ng_ref[...]
    z = z_ref[...]
    outs = []
    for h in range(B_HEADS):
        q = _l2norm(conv[:, h * hd:(h + 1) * hd]) * (hd ** -0.5)
        k = _l2norm(conv[:, B_WIDTH + h * hd:B_WIDTH + (h + 1) * hd])
        v = conv[:, 2 * B_WIDTH + h * hd:2 * B_WIDTH + (h + 1) * hd]
        beta = beta_all[:, B_HEADS + h:B_HEADS + h + 1]
        eg = eg_all[:, h:h + 1]
        qk = jnp.sum(q * k, axis=-1, keepdims=True)
        rows = []
        for b in range(GDN_S_BLOCK):
            kb = k[b:b + 1]
            s_bh = s_ref[b, h]
            kq = jnp.concatenate([kb, q[b:b + 1], zpad], axis=0).astype(BF16)
            rs = _dot(kq, s_bh.astype(BF16))
            bb = beta[b:b + 1]
            egb = eg[b:b + 1]
            v_new = bb * v[b:b + 1] - (bb * egb) * rs[0:1]
            rows.append(egb * rs[1:2] + qk[b:b + 1] * v_new)
            kdiag = jnp.where(eye, jnp.broadcast_to(kb, (hd, hd)), 0.0).astype(BF16)
            outer = _dot(kdiag, jnp.broadcast_to(v_new, (hd, hd)).astype(BF16))
            ns_ref[b, h] = s_bh * egb + outer
        o_h = jnp.concatenate(rows, axis=0)
        outs.append(_gdn_out_norm(o_h, z[:, h * hd:(h + 1) * hd], ng))
    o_ref[...] = jnp.concatenate(outs, axis=-1).astype(BF16)


def _gdn_sample(layer, conv_st, bqkv, z, ab, conv_w, alog_row, dtb_row, norm_g, s0):
    bs = GDN_S_BLOCK
    cw3 = 3 * B_WIDTH
    return pl.pallas_call(
        _gdn_sample_kernel,
        grid=(DEC_BATCH // bs,),
        in_specs=[pl.BlockSpec((CONV_W - 1, bs, cw3), lambda i: (0, i, 0)),
                  pl.BlockSpec((bs, cw3), lambda i: (i, 0)),
                  pl.BlockSpec((bs, B_WIDTH), lambda i: (i, 0)),
                  pl.BlockSpec((bs, LANES), lambda i: (i, 0)),
                  _const_spec((CONV_W, cw3)),
                  _const_spec((1, LANES)),
                  _const_spec((1, LANES)),
                  _const_spec((1, B_HEAD_DIM)),
                  pl.BlockSpec((None, bs, B_HEADS, B_HEAD_DIM, B_HEAD_DIM), lambda i: (layer, i, 0, 0, 0))],
        out_specs=[pl.BlockSpec((bs, B_WIDTH), lambda i: (i, 0)),
                   pl.BlockSpec((CONV_W - 1, bs, cw3), lambda i: (0, i, 0)),
                   pl.BlockSpec((bs, B_HEADS, B_HEAD_DIM, B_HEAD_DIM), lambda i: (i, 0, 0, 0))],
        out_shape=[jax.ShapeDtypeStruct((DEC_BATCH, B_WIDTH), BF16),
                   jax.ShapeDtypeStruct((CONV_W - 1, DEC_BATCH, cw3), F32),
                   jax.ShapeDtypeStruct((DEC_BATCH, B_HEADS, B_HEAD_DIM, B_HEAD_DIM), F32)],
        compiler_params=_params(("arbitrary",)),
        name="gdn_sample",
    )(conv_st, bqkv, z, ab, conv_w, alog_row, dtb_row, norm_g, s0)


POOL_TILE = 512
POOL_PAD = 16


def _pool_project(d, pw_ref, scale):
    outs = []
    for gi in range(POOL_GROUPS):
        lo = gi * POOL_GROUP_DIM
        outs.append(_dot(d[:, lo:lo + POOL_GROUP_DIM].astype(BF16), pw_ref[gi].astype(BF16)))
    return jnp.concatenate(outs, axis=-1) * scale


def _pool_prompt_kernel(x_ref, pw_ref, ps_ref, o_ref, buf_ref):
    n = pl.program_id(1)
    t = POOL_TILE
    pad = POOL_PAD
    gd = POOL_GROUP_DIM

    @pl.when(n == 0)
    def _():
        buf_ref[0:pad, :] = jnp.zeros((pad, C_WIDTH), F32)

    x = x_ref[...]
    buf_ref[pad:pad + t, :] = x
    s1 = buf_ref[1:pad + t, :]
    s2 = s1 + buf_ref[0:pad + t - 1, :]
    s4 = s2[2:, gd:] + s2[:-2, gd:]
    s8 = s4[4:, gd:] + s4[:-4, gd:]
    s16 = s8[8:, gd:] + s8[:-8, gd:]
    sums = (s2[15:, 0:gd], s4[13:, 0:gd], s8[9:, 0:gd], s16[1:, :])
    pos = n * t + lax.broadcasted_iota(jnp.int32, (t, 1), 0) + 1
    means = []
    for gi, w in enumerate(POOL_WINDOWS):
        cnt = jnp.minimum(pos, w).astype(F32)
        means.append(sums[gi] / cnt)
    d = jnp.concatenate(means, axis=-1) - x
    o_ref[...] = _pool_project(d, pw_ref, ps_ref[...]).astype(BF16)
    buf_ref[0:pad, :] = x[t - pad:t, :]


def _pool_prompt(cp, pool_w, pool_scale):
    nt = SEQ // POOL_TILE
    t = POOL_TILE
    return pl.pallas_call(
        _pool_prompt_kernel,
        grid=(BATCH, nt),
        in_specs=[pl.BlockSpec((t, C_WIDTH), lambda b, n: (b * nt + n, 0)),
                  _const_spec((POOL_GROUPS, POOL_GROUP_DIM, POOL_GROUP_DIM)),
                  _const_spec((1, C_WIDTH))],
        out_specs=pl.BlockSpec((t, C_WIDTH), lambda b, n: (b * nt + n, 0)),
        out_shape=jax.ShapeDtypeStruct((N_PROMPT, C_WIDTH), BF16),
        scratch_shapes=[pltpu.VMEM((POOL_PAD + t, C_WIDTH), F32)],
        compiler_params=_params(("arbitrary", "arbitrary")),
        name="pool_prompt",
    )(cp, pool_w, pool_scale)


def _pool_sample_kernel(st_ref, x_ref, pw_ref, ps_ref, o_ref, nst_ref):
    x = x_ref[...]
    gd = POOL_GROUP_DIM
    means = []
    for gi, w in enumerate(POOL_WINDOWS):
        lo = gi * gd
        acc = x[:, lo:lo + gd]
        for j in range(1, w):
            acc = acc + st_ref[POOL_HIST - j][:, lo:lo + gd]
        means.append(acc / float(min(PAST_LEN + 1, w)))
    d = jnp.concatenate(means, axis=-1) - x
    o_ref[...] = _pool_project(d, pw_ref, ps_ref[...]).astype(BF16)
    for j in range(POOL_HIST - 1):
        nst_ref[j] = st_ref[j + 1]
    nst_ref[POOL_HIST - 1] = x


def _pool_sample(pool_st, cp, pool_w, pool_scale):
    return pl.pallas_call(
        _pool_sample_kernel,
        out_shape=[jax.ShapeDtypeStruct((DEC_BATCH, C_WIDTH), BF16),
                   jax.ShapeDtypeStruct((POOL_HIST, DEC_BATCH, C_WIDTH), F32)],
        compiler_params=_params(),
        name="pool_sample",
    )(pool_st, cp, pool_w, pool_scale)


def _layer_norm(y, g, b):
    mu = jnp.mean(y, axis=-1, keepdims=True)
    yc = y - mu
    var = jnp.mean(yc * yc, axis=-1, keepdims=True)
    return yc * lax.rsqrt(var + LN_EPS) * g + b


def _store_token_tiled(ref, val, rows):
    for s in range(ROW_TILES):
        ref[pl.ds(s, rows, stride=ROW_TILES), :] = val[:, s * LANES:(s + 1) * LANES]


def _load_token_tiled(ref, base, rows):
    return jnp.concatenate([ref[pl.ds(base + s, rows, stride=ROW_TILES), :] for s in range(ROW_TILES)], axis=-1)


def _merge_kernel(hp_ref, oap_ref, obp_ref, ocp_ref, gp_ref, hs_ref, oas_ref, obs_ref, ocs_ref, gs_ref,
                  wa_ref, wb_ref, wc_ref, wo_ref, lng_ref, lnb_ref, wr_ref, br_ref,
                  h1_ref, lg_ref, *, n_prompt_tiles):
    i = pl.program_id(0)

    def run(h_ref, oa_ref, ob_ref, oc_ref, g_ref):
        rows = h_ref.shape[0]
        merged = (g_ref[:, 0:D_MODEL].astype(F32) * _dot(oa_ref[...], wa_ref[...])
                  + g_ref[:, D_MODEL:2 * D_MODEL].astype(F32) * _dot(ob_ref[...], wb_ref[...])
                  + g_ref[:, 2 * D_MODEL:3 * D_MODEL].astype(F32) * _dot(oc_ref[...], wc_ref[...]))
        mix = _dot(merged.astype(BF16), wo_ref[...])
        h1 = _layer_norm(DN_ALPHA * h_ref[...] + mix, lng_ref[...], lnb_ref[...])
        h1_ref[0:rows, :] = h1
        hh, hm, _ = _split3(h1)
        wh, wm, _ = _split3(wr_ref[...])
        lg_ref[0:rows, :] = _dot(hh, wh) + _dot(hh, wm) + _dot(hm, wh) + br_ref[...]

    @pl.when(i < n_prompt_tiles)
    def _():
        run(hp_ref, oap_ref, obp_ref, ocp_ref, gp_ref)

    @pl.when(i == n_prompt_tiles)
    def _():
        run(hs_ref, oas_ref, obs_ref, ocs_ref, gs_ref)


def _merge(prompt, sample, wa, wb, wc, wo, lng, lnb, wr, br, tm):
    n_p = N_PROMPT // tm
    last = n_p - 1
    widths = (D_MODEL, A_WIDTH, B_WIDTH, C_WIDTH, N_BRANCHES * D_MODEL)
    in_specs = ([pl.BlockSpec((tm, w), lambda i: (jnp.minimum(i, last), 0)) for w in widths]
                + [_const_spec((DEC_BATCH, w)) for w in widths]
                + [_const_spec((A_WIDTH, D_MODEL)), _const_spec((B_WIDTH, D_MODEL)), _const_spec((C_WIDTH, D_MODEL)),
                   _const_spec((D_MODEL, D_MODEL)), _const_spec((1, D_MODEL)), _const_spec((1, D_MODEL)),
                   _const_spec((D_MODEL, LANES)), _const_spec((1, LANES))])
    return pl.pallas_call(
        functools.partial(_merge_kernel, n_prompt_tiles=n_p),
        grid=(n_p + 1,),
        in_specs=in_specs,
        out_specs=[pl.BlockSpec((tm, D_MODEL), lambda i: (i, 0)),
                   pl.BlockSpec((tm, LANES), lambda i: (i, 0))],
        out_shape=[jax.ShapeDtypeStruct((N_ROWS, D_MODEL), F32),
                   jax.ShapeDtypeStruct((N_ROWS, LANES), F32)],
        compiler_params=_params(("arbitrary",)),
        name="merge",
    )(*prompt, *sample, wa, wb, wc, wo, lng, lnb, wr, br)


ROUTE_TILE = 384
REC_EXPERT = 0
REC_RANK = 2
REC_WEIGHT = 4


def _route_kernel(lg_ref, rec_ref, sizes_ref, carry_ref):
    i = pl.program_id(0)
    t_rows = ROUTE_TILE

    @pl.when(i == 0)
    def _():
        carry_ref[...] = jnp.zeros_like(carry_ref)

    lg = lg_ref[...]
    lane = lax.broadcasted_iota(jnp.int32, (t_rows, LANES), 1).astype(F32)
    far = float(LANES)

    def masked_argmax(mask):
        v = jnp.max(jnp.where(mask, lg, NEG_INF), axis=-1, keepdims=True)
        idx = jnp.min(jnp.where(mask & (lg == v), lane, far), axis=-1, keepdims=True)
        return v, idx

    gmask = lane < N_GROUPS
    gmax, gsel = masked_argmax(gmask)
    gw = 1.0 / jnp.sum(jnp.where(gmask, jnp.exp(lg - gmax), 0.0), axis=-1, keepdims=True)
    lo = N_GROUPS + EXPERTS_PER_GROUP * gsel
    emask = (lane >= lo) & (lane < lo + EXPERTS_PER_GROUP)
    v1, i1 = masked_argmax(emask)
    v2, i2 = masked_argmax(emask & (lane != i1))
    t = jnp.exp(v2 - v1)
    w1 = gw / (1.0 + t)
    w2 = gw * t / (1.0 + t)
    e1 = i1 - N_GROUPS
    e2 = i2 - N_GROUPS

    o1 = jnp.where(lane == e1, 1.0, 0.0)
    o2 = jnp.where(lane == e2, 1.0, 0.0)
    r = lax.broadcasted_iota(jnp.int32, (t_rows, t_rows), 0)
    c = lax.broadcasted_iota(jnp.int32, (t_rows, t_rows), 1)
    before = jnp.where(c < r, 1.0, 0.0).astype(BF16)
    carry = carry_ref[...]
    tot1 = jnp.sum(o1, axis=0, keepdims=True)
    tot2 = jnp.sum(o2, axis=0, keepdims=True)
    rank1 = jnp.sum(o1 * (_dot(before, o1.astype(BF16)) + carry), axis=-1, keepdims=True)
    rank2 = jnp.sum(o2 * (_dot(before, o2.astype(BF16)) + (carry + tot1)), axis=-1, keepdims=True)
    carry = carry + tot1 + tot2
    carry_ref[...] = carry
    sizes_ref[...] = carry

    rec = jnp.zeros((t_rows, LANES), F32)
    for k, val in enumerate((e1, e2, rank1, rank2, w1, w2)):
        rec = jnp.where(lane == k, val, rec)
    rec_ref[...] = rec


def _route(logits):
    rec, sizes_row = pl.pallas_call(
        _route_kernel,
        grid=(N_ROWS // ROUTE_TILE,),
        in_specs=[pl.BlockSpec((ROUTE_TILE, LANES), lambda i: (i, 0))],
        out_specs=[pl.BlockSpec((ROUTE_TILE, LANES), lambda i: (i, 0)), _const_spec((1, LANES))],
        out_shape=[jax.ShapeDtypeStruct((N_ROWS, LANES), F32), jax.ShapeDtypeStruct((1, LANES), F32)],
        scratch_shapes=[pltpu.VMEM((1, LANES), F32)],
        compiler_params=_params(("arbitrary",)),
        name="route",
    )(logits)
    eid = rec[:, REC_EXPERT:REC_EXPERT + TOP_K].astype(jnp.int32)
    rank = rec[:, REC_RANK:REC_RANK + TOP_K].astype(jnp.int32)
    sizes = sizes_row[0, :N_EXPERTS].astype(jnp.int32)
    padded = (sizes + MOE_ROWS - 1) // MOE_ROWS * MOE_ROWS
    end_padded = jnp.cumsum(padded).astype(jnp.int32)
    start_padded = end_padded - padded
    experts = jnp.arange(N_EXPERTS, dtype=jnp.int32)
    dest = jnp.sum(jnp.where(eid[:, :, None] == experts, start_padded, 0), axis=-1) + rank
    blk_start = jnp.arange(MOE_BLOCKS, dtype=jnp.int32) * MOE_ROWS
    n_used = end_padded[-1] // MOE_ROWS
    blk_exp = jnp.sum((blk_start[:, None] >= end_padded[None, :]).astype(jnp.int32), axis=-1)
    last_exp = jnp.sum((blk_start[jnp.maximum(n_used - 1, 0)] >= end_padded).astype(jnp.int32))
    blk_exp = jnp.where(blk_start < end_padded[-1], blk_exp, last_exp).astype(jnp.int32)
    return rec, dest, end_padded, padded.astype(jnp.int32), blk_exp, n_used.reshape(1).astype(jnp.int32)


DMA_UNROLL = 16


def _tile_rows(ref, j):
    return ref.at[pl.ds(pl.multiple_of(j * ROW_TILES, ROW_TILES), ROW_TILES), :]


def _gather_rows(idx_ref, src_ref, dst_ref, sem, count):
    def body(jj, carry):
        for p in range(DMA_UNROLL):
            j = DMA_UNROLL * jj + p
            pltpu.make_async_copy(_tile_rows(src_ref, idx_ref[0, 0, j]), _tile_rows(dst_ref, j),
                                  sem).start(priority=p % 2)
        return carry
    lax.fori_loop(0, count // DMA_UNROLL, body, 0)


def _gather_wait(src_ref, dst_ref, sem, count):
    pltpu.make_async_copy(src_ref.at[pl.ds(0, count * ROW_TILES), :], dst_ref, sem).wait()


DISPATCH_TILE = 384


def _dispatch_kernel(endp_ref, padded_ref, nused_ref, idx_ref, h_ref, xb_hbm, stage_ref, zero_ref, sem_ref, zsem_ref):
    i = pl.program_id(0)
    n = pl.num_programs(0)
    tm = h_ref.shape[0]
    slot = i % 2
    blk = MOE_ROWS * ROW_TILES

    def zero_copy(b0):
        return pltpu.make_async_copy(zero_ref, xb_hbm.at[pl.ds(pl.multiple_of(b0 * ROW_TILES, ROW_TILES), blk), :], zsem_ref.at[0])

    @pl.when(i == 0)
    def _():
        zero_ref[...] = jnp.zeros_like(zero_ref)
        for e in range(N_EXPERTS):
            @pl.when(padded_ref[e] > 0)
            def _():
                zero_copy(endp_ref[e] - MOE_ROWS).start()

        def tail_start(b, carry):
            zero_copy(b * MOE_ROWS).start()
            return carry
        lax.fori_loop(nused_ref[0], MOE_BLOCKS, tail_start, 0)
        for e in range(N_EXPERTS):
            @pl.when(padded_ref[e] > 0)
            def _():
                zero_copy(endp_ref[e] - MOE_ROWS).wait()

        def tail_wait(b, carry):
            zero_copy(b * MOE_ROWS).wait()
            return carry
        lax.fori_loop(nused_ref[0], MOE_BLOCKS, tail_wait, 0)

    def wait_slot(s):
        for _ in range(TOP_K):
            pltpu.make_async_copy(stage_ref.at[s], xb_hbm.at[pl.ds(0, tm * ROW_TILES), :], sem_ref.at[s]).wait()

    @pl.when(i >= 2)
    def _():
        wait_slot(slot)

    stage = stage_ref.at[slot]
    _store_token_tiled(stage, h_ref[...], tm)

    def body(tt, carry):
        for u in range(DMA_UNROLL // TOP_K):
            t = (DMA_UNROLL // TOP_K) * tt + u
            for k in range(TOP_K):
                pltpu.make_async_copy(_tile_rows(stage, t), _tile_rows(xb_hbm, idx_ref[0, 0, k * tm + t]),
                                      sem_ref.at[slot]).start(priority=k)
        return carry
    lax.fori_loop(0, tm // (DMA_UNROLL // TOP_K), body, 0)

    @pl.when(i == n - 1)
    def _():
        wait_slot(slot)

        @pl.when(n > 1)
        def _():
            wait_slot(1 - slot)


def _tile_major_idx(dest, n_tiles, tm):
    idx = dest.reshape(n_tiles, tm, TOP_K)
    return jnp.transpose(idx, (0, 2, 1)).reshape(n_tiles, 1, TOP_K * tm)


def _dispatch(end_padded, padded, n_used, dest, h1):
    tm = DISPATCH_TILE
    n_tiles = N_ROWS // tm
    idx = _tile_major_idx(dest, n_tiles, tm)
    grid_spec = pltpu.PrefetchScalarGridSpec(
        num_scalar_prefetch=3,
        grid=(n_tiles,),
        in_specs=[pl.BlockSpec((1, 1, TOP_K * tm), lambda i, *_: (i, 0, 0), memory_space=pltpu.SMEM),
                  pl.BlockSpec((tm, D_MODEL), lambda i, *_: (i, 0))],
        out_specs=pl.BlockSpec(memory_space=pl.ANY),
        scratch_shapes=[pltpu.VMEM((2, tm * ROW_TILES, LANES), F32),
                        pltpu.VMEM((MOE_ROWS * ROW_TILES, LANES), F32),
                        pltpu.SemaphoreType.DMA((2,)),
                        pltpu.SemaphoreType.DMA((1,))])
    return pl.pallas_call(
        _dispatch_kernel,
        grid_spec=grid_spec,
        out_shape=jax.ShapeDtypeStruct((MOE_CAP * ROW_TILES, LANES), F32),
        compiler_params=_params(("arbitrary",)),
        name="dispatch",
    )(end_padded, padded, n_used, idx, h1)


def _expert_kernel(bexp_ref, nused_ref, x_ref, wg_ref, wu_ref, wd_ref, y_ref, wg16_ref, wu16_ref, wd16_ref):
    i = pl.program_id(0)
    n_used = nused_ref[0]
    rows = MOE_ROWS

    @pl.when((i == 0) | (bexp_ref[i] != bexp_ref[jnp.maximum(i - 1, 0)]))
    def _():
        wg16_ref[...] = wg_ref[...].astype(BF16)
        wu16_ref[...] = wu_ref[...].astype(BF16)
        wd16_ref[...] = wd_ref[...].astype(BF16)

    @pl.when(i < n_used)
    def _():
        x = _load_token_tiled(x_ref, 0, rows).astype(BF16)
        hid = _silu(_dot(x, wg16_ref[...])) * _dot(x, wu16_ref[...])
        _store_token_tiled(y_ref, _dot(hid.astype(BF16), wd16_ref[...]), rows)

    @pl.when(i >= n_used)
    def _():
        y_ref[...] = jnp.zeros_like(y_ref)


def _experts(layer, blk_exp, n_used, xb, w_gate, w_up, w_down):
    rows = MOE_ROWS
    blk = (rows * ROW_TILES, LANES)
    grid_spec = pltpu.PrefetchScalarGridSpec(
        num_scalar_prefetch=2,
        grid=(MOE_BLOCKS,),
        in_specs=[pl.BlockSpec(blk, lambda i, be, nu: (jnp.minimum(i, nu[0] - 1), 0)),
                  pl.BlockSpec((None, None, D_MODEL, EXPERT_FF), lambda i, be, nu: (layer, be[i], 0, 0)),
                  pl.BlockSpec((None, None, D_MODEL, EXPERT_FF), lambda i, be, nu: (layer, be[i], 0, 0)),
                  pl.BlockSpec((None, None, EXPERT_FF, D_MODEL), lambda i, be, nu: (layer, be[i], 0, 0))],
        out_specs=pl.BlockSpec(blk, lambda i, be, nu: (i, 0)),
        scratch_shapes=[pltpu.VMEM((D_MODEL, EXPERT_FF), BF16),
                        pltpu.VMEM((D_MODEL, EXPERT_FF), BF16),
                        pltpu.VMEM((EXPERT_FF, D_MODEL), BF16)])
    return pl.pallas_call(
        _expert_kernel,
        grid_spec=grid_spec,
        out_shape=jax.ShapeDtypeStruct((MOE_CAP * ROW_TILES, LANES), F32),
        compiler_params=_params(("arbitrary",)),
        name="experts",
    )(blk_exp, n_used, xb, w_gate, w_up, w_down)


def _combine_kernel(idx_ref, idxn_ref, h1_ref, rec_ref, y_hbm, lng_ref, lnb_ref, o_ref, ybuf_ref, sem_ref, *, n_tiles):
    i = pl.program_id(0)
    tm = h1_ref.shape[0]
    slot = i % 2
    count = TOP_K * tm

    @pl.when(i == 0)
    def _():
        _gather_rows(idx_ref, y_hbm, ybuf_ref.at[0], sem_ref.at[0], count)

    @pl.when(i + 1 < n_tiles)
    def _():
        _gather_rows(idxn_ref, y_hbm, ybuf_ref.at[1 - slot], sem_ref.at[1 - slot], count)

    _gather_wait(y_hbm, ybuf_ref.at[slot], sem_ref.at[slot], count)
    y = DN_ALPHA * h1_ref[...]
    for k in range(TOP_K):
        wk = rec_ref[:, REC_WEIGHT + k:REC_WEIGHT + k + 1]
        y = y + wk * _load_token_tiled(ybuf_ref.at[slot], k * tm * ROW_TILES, tm)
    o_ref[...] = _layer_norm(y, lng_ref[...], lnb_ref[...])


def _combine(dest, rec, h1, yb, lng, lnb, tm, row0, n):
    n_tiles = n // tm
    blk0 = row0 // tm
    idx = _tile_major_idx(dest[row0:row0 + n], n_tiles, tm)
    last = n_tiles - 1
    return pl.pallas_call(
        functools.partial(_combine_kernel, n_tiles=n_tiles),
        grid=(n_tiles,),
        in_specs=[pl.BlockSpec((1, 1, TOP_K * tm), lambda i: (i, 0, 0), memory_space=pltpu.SMEM),
                  pl.BlockSpec((1, 1, TOP_K * tm), lambda i: (jnp.minimum(i + 1, last), 0, 0),
                               memory_space=pltpu.SMEM),
                  pl.BlockSpec((tm, D_MODEL), lambda i: (blk0 + i, 0)),
                  pl.BlockSpec((tm, LANES), lambda i: (blk0 + i, 0)),
                  pl.BlockSpec(memory_space=pl.ANY),
                  _const_spec((1, D_MODEL)), _const_spec((1, D_MODEL))],
        out_specs=pl.BlockSpec((tm, D_MODEL), lambda i: (i, 0)),
        out_shape=jax.ShapeDtypeStruct((n, D_MODEL), F32),
        scratch_shapes=[pltpu.VMEM((2, TOP_K * tm * ROW_TILES, LANES), F32),
                        pltpu.SemaphoreType.DMA((2,))],
        compiler_params=_params(("arbitrary",)),
        name="combine",
    )(idx, idx, h1, rec, yb, lng, lnb)


PROMPT_TILE = 512
COMBINE_TILE = 256


def _row(v, width=None):
    v = v.reshape(1, -1).astype(F32)
    if width is not None and v.shape[1] < width:
        v = jnp.pad(v, ((0, 0), (0, width - v.shape[1])))
    return v


def _layer(layer, hp, hs, ck, cv, conv_st, rec_all, pool_st, lw, w_packed, experts_w):
    alog_row = _row(lw['gdn_a_log'], LANES)
    dtb_row = _row(lw['gdn_dt_bias'], LANES)
    norm_g = _row(lw['gdn_norm_g'])
    pool_scale = _row(lw['pool_scale'])
    sinks = lw['attn_sinks'].astype(F32)
    kvw = A_KV_HEADS * A_HEAD_DIM

    gates_p, bqkv_p, aq_p, z_p, cp_p, kv_p, ab_p = _in_proj(layer, hp, w_packed, PROMPT_TILE)
    oa_p = _attn_prompt(sinks, aq_p, kv_p)
    ob_p, rec_p = _gdn_prompt(bqkv_p, z_p, ab_p, lw['gdn_conv_w'], alog_row, dtb_row, norm_g)
    oc_p = _pool_prompt(cp_p, lw['pool_w'], pool_scale)
    kv_tail = kv_p.reshape(BATCH, SEQ, 2, A_KV_HEADS, A_HEAD_DIM)[:, SEQ - WINDOW:]
    new_k_p, new_v_p = kv_tail[:, :, 0], kv_tail[:, :, 1]
    new_conv_p = bqkv_p.reshape(BATCH, SEQ, 3 * B_WIDTH)[:, SEQ - (CONV_W - 1):]
    new_pool_p = cp_p.reshape(BATCH, SEQ, C_WIDTH)[:, SEQ - POOL_HIST:]

    gates_s, bqkv_s, aq_s, z_s, cp_s, kv_s, ab_s = _in_proj(layer, hs, w_packed, DEC_BATCH)
    oa_s3, nk_s, nv_s = _attn_sample(layer, sinks, aq_s.astype(F32).reshape(DEC_BATCH, A_HEADS, A_HEAD_DIM),
                                     kv_s[:, :kvw], kv_s[:, kvw:], ck, cv)
    oa_s = oa_s3.reshape(DEC_BATCH, A_WIDTH).astype(BF16)
    ob_s, nconv_s, rec_s = _gdn_sample(layer, jnp.transpose(conv_st, (1, 0, 2)), bqkv_s, z_s, ab_s, lw['gdn_conv_w'],
                                       alog_row, dtb_row, norm_g, rec_all)
    oc_s, npool_s = _pool_sample(jnp.transpose(pool_st, (1, 0, 2)), cp_s, lw['pool_w'], pool_scale)
    new_k_s = nk_s.reshape(DEC_BATCH, WINDOW, A_KV_HEADS, A_HEAD_DIM)
    new_v_s = nv_s.reshape(DEC_BATCH, WINDOW, A_KV_HEADS, A_HEAD_DIM)
    new_conv_s = jnp.transpose(nconv_s, (1, 0, 2))
    new_pool_s = jnp.transpose(npool_s, (1, 0, 2))

    wa, wb, wc = (lw[k].astype(BF16) for k in ('w_branch_a', 'w_branch_b', 'w_branch_c'))
    wo = lw['w_o'].astype(BF16)
    wr = jnp.concatenate([lw['router_group_w'],
                          jnp.transpose(lw['router_expert_w'], (1, 0, 2)).reshape(D_MODEL, N_EXPERTS),
                          jnp.zeros((D_MODEL, LANES - ROUTER_COLS), F32)], axis=-1)
    br = _row(jnp.concatenate([lw['router_group_b'], lw['router_expert_b'].reshape(-1)]), LANES)
    ln1g, ln1b, ln2g, ln2b = (_row(lw[k]) for k in ('ln1_g', 'ln1_b', 'ln2_g', 'ln2_b'))
    h1, logits = _merge((hp, oa_p, ob_p, oc_p, gates_p), (hs, oa_s, ob_s, oc_s, gates_s),
                        wa, wb, wc, wo, ln1g, ln1b, wr, br, PROMPT_TILE)
    rec, dest, end_padded, padded, blk_exp, n_used = _route(logits)
    xb = _dispatch(end_padded, padded, n_used, dest, h1)
    yb = _experts(layer, blk_exp, n_used, xb, *experts_w)
    h2_p = _combine(dest, rec, h1, yb, ln2g, ln2b, COMBINE_TILE, 0, N_PROMPT)
    h2_s = _combine(dest, rec, h1, yb, ln2g, ln2b, DEC_BATCH, N_PROMPT, DEC_BATCH)
    return (h2_p, h2_s, (new_k_p, new_v_p, new_conv_p, rec_p, new_pool_p),
            (new_k_s, new_v_s, new_conv_s, rec_s, new_pool_s))


def kernel(x_prompt, x_sample, cache_attn_k, cache_attn_v, state_gdn_conv, state_gdn_rec, state_pool, w_in, attn_sinks, gdn_conv_w, gdn_a_log, gdn_dt_bias, gdn_norm_g, pool_w, pool_scale, w_branch_a, w_branch_b, w_branch_c, w_o, ln1_g, ln1_b, router_group_w, router_group_b, router_expert_w, router_expert_b, w_gate, w_up, w_down, ln2_g, ln2_b):
    weights = dict(attn_sinks=attn_sinks, gdn_conv_w=gdn_conv_w, gdn_a_log=gdn_a_log,
                   gdn_dt_bias=gdn_dt_bias, gdn_norm_g=gdn_norm_g, pool_w=pool_w, pool_scale=pool_scale,
                   w_branch_a=w_branch_a, w_branch_b=w_branch_b, w_branch_c=w_branch_c, w_o=w_o,
                   ln1_g=ln1_g, ln1_b=ln1_b, router_group_w=router_group_w, router_group_b=router_group_b,
                   router_expert_w=router_expert_w, router_expert_b=router_expert_b,
                   ln2_g=ln2_g, ln2_b=ln2_b)
    hp = x_prompt.reshape(N_PROMPT, D_MODEL)
    hs = x_sample.reshape(DEC_BATCH, D_MODEL)
    kvw = A_KV_HEADS * A_HEAD_DIM
    ck = cache_attn_k.reshape(DEPTH, DEC_BATCH, WINDOW, kvw)
    cv = cache_attn_v.reshape(DEPTH, DEC_BATCH, WINDOW, kvw)
    w_packed = _pack_w_in(w_in)
    st_p = [[] for _ in range(5)]
    st_s = [[] for _ in range(5)]
    for l in range(DEPTH):
        lw = {k: v[l] for k, v in weights.items()}
        hp, hs, sp, ss = _layer(l, hp, hs, ck, cv, state_gdn_conv[l], state_gdn_rec, state_pool[l], lw,
                                w_packed, (w_gate, w_up, w_down))
        for j in range(5):
            st_p[j].append(sp[j])
            st_s[j].append(ss[j])
    return (hp.reshape(BATCH, SEQ, D_MODEL), hs.reshape(DEC_BATCH, 1, D_MODEL),
            *(jnp.stack(t) for t in st_p), *(jnp.stack(t) for t in st_s))
```
